```python
import math
import jax, jax.numpy as jnp
from jax import lax
import numpy as np

D_MODEL = 4096
BATCH = 16
SEQ = 256
DEPTH = 2
DEC_BATCH = 4
DEC_SEQ = 2048
PAST_LEN = 256

GRID_W = 64
HEAD_DIM = 128
D_MIX = D_MODEL
GROUP_W = D_MIX // 4
A_HEADS = GROUP_W // HEAD_DIM
A_KH = 8
A_KW = 16
B_HEADS = GROUP_W // HEAD_DIM
B_NOPE = 128
B_ROPE = 64
B_V = 128
B_Q_LORA = 3 * D_MODEL // 16
B_KV_LORA = 512
C_HEADS = GROUP_W // HEAD_DIM
C_KV_HEADS = 2
D_HEADS = GROUP_W // HEAD_DIM
D_QK = HEAD_DIM // 2
D_V = HEAD_DIM
ROPE_THETA = 10000.0
Q_BLOCK = 128
EPS = 1e-6
IN_SIZES = (A_HEADS * HEAD_DIM, A_HEADS * HEAD_DIM, A_HEADS * HEAD_DIM, GROUP_W,
            B_Q_LORA, B_KV_LORA, B_ROPE, GROUP_W,
            C_HEADS * HEAD_DIM, C_KV_HEADS * HEAD_DIM, C_KV_HEADS * HEAD_DIM, GROUP_W,
            D_HEADS * 2 * D_QK, D_HEADS * 2 * D_QK, D_HEADS * D_V, GROUP_W)
IN_COLS = sum(IN_SIZES)

kernel_name = 'hybrid_headgroup_diffusion_step'


def rmsnorm(x, g):
    xf = x.astype(jnp.float32)
    y = xf * lax.rsqrt(jnp.mean(xf * xf, axis=-1, keepdims=True) + EPS)
    return (y * g.astype(jnp.float32)).astype(x.dtype)


def rope_1d(x, pos):
    d = x.shape[-1]
    half = d // 2
    freqs = ROPE_THETA ** (-(jnp.arange(half, dtype=jnp.float32) * 2.0 / d))
    ang = pos.astype(jnp.float32)[:, None] * freqs[None, :]
    shape = (1, x.shape[1]) + (1,) * (x.ndim - 3) + (half,)
    cos = jnp.cos(ang).reshape(shape)
    sin = jnp.sin(ang).reshape(shape)
    xf = x.astype(jnp.float32)
    x1, x2 = xf[..., :half], xf[..., half:]
    return jnp.concatenate([x1 * cos - x2 * sin, x2 * cos + x1 * sin], axis=-1).astype(x.dtype)


def axial_rope(x):
    t = jnp.arange(x.shape[1])
    h = x.shape[-1] // 2
    return jnp.concatenate([rope_1d(x[..., :h], t // GRID_W), rope_1d(x[..., h:], t % GRID_W)], axis=-1)


def block_attn(q, k, v, scale):
    B, T, G, R, Dk = q.shape
    nb = T // Q_BLOCK
    qb = jnp.moveaxis(q.reshape(B, nb, Q_BLOCK, G, R, Dk), 1, 0)

    def one(qi):
        s = jnp.einsum('bqgrd,bsgd->bgrqs', qi, k, preferred_element_type=jnp.float32) * scale
        p = jax.nn.softmax(s, axis=-1)
        return jnp.einsum('bgrqs,bsge->bqgre', p.astype(v.dtype), v)

    o = lax.map(one, qb)
    return jnp.moveaxis(o, 0, 1).reshape(B, T, G, R, v.shape[-1])


def natten_attn(q, k, v, k_ctx, v_ctx, rpb):
    B, T, H, Dh = q.shape
    rows = T // GRID_W
    kh = min(A_KH, rows)
    r = jnp.arange(rows)
    start_r = jnp.clip(r - kh // 2, 0, rows - kh)
    row_idx = start_r[:, None] + jnp.arange(kh)[None, :]
    col = jnp.arange(GRID_W)
    start_c = jnp.clip(col - A_KW // 2, 0, GRID_W - A_KW)
    col_ok = (col[None, :] >= start_c[:, None]) & (col[None, :] < start_c[:, None] + A_KW)
    dr_idx = row_idx - r[:, None] + (A_KH - 1)
    dc_idx = jnp.clip(col[None, :] - col[:, None], -(A_KW - 1), A_KW - 1) + (A_KW - 1)
    bias = rpb.astype(jnp.float32)[:, dr_idx[:, None, :, None], dc_idx[None, :, None, :]]
    scale = Dh ** -0.5
    qg = q.reshape(B, rows, GRID_W, H, Dh)
    kg = k.reshape(B, rows, GRID_W, H, Dh)[:, row_idx]
    vg = v.reshape(B, rows, GRID_W, H, Dh)[:, row_idx]
    s_loc = jnp.einsum('brqhd,brkwhd->bhrqkw', qg, kg, preferred_element_type=jnp.float32) * scale
    s_loc = jnp.where(col_ok[None, None, None, :, None, :], s_loc + bias[None], -jnp.inf)
    s_ctx = jnp.einsum('brqhd,bshd->bhrqs', qg, k_ctx, preferred_element_type=jnp.float32) * scale
    nloc = kh * GRID_W
    s = jnp.concatenate([s_loc.reshape(B, H, rows, GRID_W, nloc), s_ctx], axis=-1)
    p = jax.nn.softmax(s, axis=-1).astype(v.dtype)
    p_loc = p[..., :nloc].reshape(B, H, rows, GRID_W, kh, GRID_W)
    p_ctx = p[..., nloc:]
    o = jnp.einsum('bhrqkw,brkwhd->brqhd', p_loc, vg) + jnp.einsum('bhrqs,bshd->brqhd', p_ctx, v_ctx)
    return o.reshape(B, T, H * Dh)


def mla_expand(ckv, kpe, w_ukv):
    B, S, _ = ckv.shape
    kv = jnp.einsum('bsc,ce->bse', ckv, w_ukv).reshape(B, S, B_HEADS, B_NOPE + B_V)
    k_nope, v = kv[..., :B_NOPE], kv[..., B_NOPE:]
    k_pe = jnp.broadcast_to(kpe[:, :, None, :], (B, S, B_HEADS, B_ROPE))
    return jnp.concatenate([k_nope, k_pe], axis=-1), v


def mla_attn(q, k, v):
    B, T = q.shape[:2]
    return block_attn(q[:, :, :, None], k, v, (B_NOPE + B_ROPE) ** -0.5).reshape(B, T, B_HEADS * B_V)


def gqa_attn(q, k, v):
    B, T = q.shape[:2]
    qg = q.reshape(B, T, C_KV_HEADS, C_HEADS // C_KV_HEADS, HEAD_DIM)
    return block_attn(qg, k, v, HEAD_DIM ** -0.5).reshape(B, T, C_HEADS * HEAD_DIM)


def diff_attn(q, k, v, lp, lam_init):
    B, T = q.shape[:2]
    S = k.shape[1]
    o = block_attn(q.reshape(B, T, D_HEADS * 2, 1, D_QK), k.reshape(B, S, D_HEADS * 2, D_QK),
                   jnp.repeat(v, 2, axis=2), D_QK ** -0.5)
    o = o.reshape(B, T, D_HEADS, 2, D_V).astype(jnp.float32)
    f32 = jnp.float32
    lam = (jnp.exp(jnp.sum(lp['d_lq1'].astype(f32) * lp['d_lk1'].astype(f32)))
           - jnp.exp(jnp.sum(lp['d_lq2'].astype(f32) * lp['d_lk2'].astype(f32))) + lam_init)
    od = o[..., 0, :] - lam * o[..., 1, :]
    od = rmsnorm(od, lp['d_subln']) * (1.0 - lam_init)
    return od.reshape(B, T, D_HEADS * D_V).astype(q.dtype)


def project(h, lp):
    B, T, _ = h.shape
    z = jnp.einsum('btd,de->bte', h, lp['w_in'])
    offsets = np.cumsum(IN_SIZES)[:-1].tolist()
    (a_q, a_k, a_v, a_z, b_cq, b_ckv, b_kpe, b_z,
     c_q, c_k, c_v, c_z, d_q, d_k, d_v, d_z) = jnp.split(z, offsets, axis=-1)
    b_q = jnp.einsum('btc,ce->bte', rmsnorm(b_cq, lp['b_q_norm']), lp['b_w_uq'])
    qs = (a_q.reshape(B, T, A_HEADS, HEAD_DIM),
          b_q.reshape(B, T, B_HEADS, B_NOPE + B_ROPE),
          rmsnorm(c_q.reshape(B, T, C_HEADS, HEAD_DIM), lp['c_q_norm']),
          d_q.reshape(B, T, D_HEADS, 2, D_QK))
    kvs = (a_k.reshape(B, T, A_HEADS, HEAD_DIM), a_v.reshape(B, T, A_HEADS, HEAD_DIM),
           rmsnorm(b_ckv, lp['b_kv_norm']), b_kpe,
           rmsnorm(c_k.reshape(B, T, C_KV_HEADS, HEAD_DIM), lp['c_k_norm']),
           c_v.reshape(B, T, C_KV_HEADS, HEAD_DIM),
           d_k.reshape(B, T, D_HEADS, 2, D_QK), d_v.reshape(B, T, D_HEADS, D_V))
    gates = (a_z, b_z, c_z, d_z)
    return qs, kvs, gates


def merge(outs, gates, w_out):
    y = jnp.concatenate([o * jax.nn.silu(z) for o, z in zip(outs, gates)], axis=-1)
    return jnp.einsum('bte,ed->btd', y, w_out)


def mix_context(h, lp, lam_init):
    B, T = h.shape[:2]
    qs, kvs, gates = project(h, lp)
    q_a, q_b, q_c, q_d = qs
    a_k, a_v, b_ckv, b_kpe, c_k, c_v, d_k, d_v = kvs
    o_a = block_attn(q_a[:, :, :, None], a_k, a_v, HEAD_DIM ** -0.5).reshape(B, T, A_HEADS * HEAD_DIM)
    k_b, v_b = mla_expand(b_ckv, b_kpe, lp['b_w_ukv'])
    o_b = mla_attn(q_b, k_b, v_b)
    o_c = gqa_attn(q_c, c_k, c_v)
    o_d = diff_attn(q_d, d_k, d_v, lp, lam_init)
    return merge((o_a, o_b, o_c, o_d), gates, lp['w_out']), kvs


def mix_latent(h, lp, ctx, lam_init):
    qs, kvs, gates = project(h, lp)
    q_a, q_b, q_c, q_d = qs
    a_k, a_v, b_ckv, b_kpe, c_k, c_v, d_k, d_v = kvs
    ca_k, ca_v, cb_ckv, cb_kpe, cc_k, cc_v, cd_k, cd_v = ctx
    o_a = natten_attn(q_a, a_k, a_v, ca_k, ca_v, lp['a_rpb'])
    q_b = jnp.concatenate([q_b[..., :B_NOPE], axial_rope(q_b[..., B_NOPE:])], axis=-1)
    k_bl, v_bl = mla_expand(b_ckv, axial_rope(b_kpe), lp['b_w_ukv'])
    k_bc, v_bc = mla_expand(cb_ckv, cb_kpe, lp['b_w_ukv'])
    o_b = mla_attn(q_b, jnp.concatenate([k_bl, k_bc], axis=1), jnp.concatenate([v_bl, v_bc], axis=1))
    o_c = gqa_attn(axial_rope(q_c), jnp.concatenate([axial_rope(c_k), cc_k], axis=1),
                   jnp.concatenate([c_v, cc_v], axis=1))
    o_d = diff_attn(axial_rope(q_d), jnp.concatenate([axial_rope(d_k), cd_k], axis=1),
                    jnp.concatenate([d_v, cd_v], axis=1), lp, lam_init)
    return merge((o_a, o_b, o_c, o_d), gates, lp['w_out'])


def modulate(x, g, mod):
    shift, scale, gate = jnp.split(mod, 3, axis=-1)
    return rmsnorm(x, g) * (1.0 + scale) + shift, gate


def setup_inputs(seed: int = 0) -> dict:
    key = jax.random.key(seed)
    ks = jax.random.split(key, 32)
    f32 = jnp.float32

    def nrm(k, shape, s=1.0):
        return jax.random.normal(k, shape, f32) * s

    def gain(k, shape):
        return 1.0 + 0.02 * jax.random.normal(k, shape, f32)

    return {
        'x_prompt': nrm(ks[0], (BATCH, SEQ, D_MODEL)),
        'x_sample': nrm(ks[1], (DEC_BATCH, DEC_SEQ, D_MODEL)),
        'cache_a_k': nrm(ks[2], (DEC_BATCH, DEPTH, PAST_LEN, A_HEADS, HEAD_DIM)),
        'cache_a_v': nrm(ks[3], (DEC_BATCH, DEPTH, PAST_LEN, A_HEADS, HEAD_DIM)),
        'cache_b_ckv': nrm(ks[4], (DEC_BATCH, DEPTH, PAST_LEN, B_KV_LORA)),
        'cache_b_kpe': nrm(ks[5], (DEC_BATCH, DEPTH, PAST_LEN, B_ROPE)),
        'cache_c_k': nrm(ks[6], (DEC_BATCH, DEPTH, PAST_LEN, C_KV_HEADS, HEAD_DIM)),
        'cache_c_v': nrm(ks[7], (DEC_BATCH, DEPTH, PAST_LEN, C_KV_HEADS, HEAD_DIM)),
        'cache_d_k': nrm(ks[8], (DEC_BATCH, DEPTH, PAST_LEN, D_HEADS, 2, D_QK)),
        'cache_d_v': nrm(ks[9], (DEC_BATCH, DEPTH, PAST_LEN, D_HEADS, D_V)),
        'c': nrm(ks[10], (DEC_BATCH, D_MODEL)),
        'c_ctx': nrm(ks[11], (D_MODEL,)),
        'norm_g': gain(ks[12], (DEPTH, D_MODEL)),
        'w_ada': nrm(ks[13], (DEPTH, D_MODEL, 3 * D_MODEL), 0.5 * D_MODEL ** -0.5),
        'b_ada': nrm(ks[14], (DEPTH, 3 * D_MODEL), 0.02),
        'w_in': nrm(ks[15], (DEPTH, D_MODEL, IN_COLS), D_MODEL ** -0.5),
        'w_out': nrm(ks[16], (DEPTH, D_MIX, D_MODEL), D_MIX ** -0.5),
        'a_rpb': nrm(ks[17], (DEPTH, A_HEADS, 2 * A_KH - 1, 2 * A_KW - 1), 0.1),
        'b_q_norm': gain(ks[18], (DEPTH, B_Q_LORA)),
        'b_w_uq': nrm(ks[19], (DEPTH, B_Q_LORA, B_HEADS * (B_NOPE + B_ROPE)), B_Q_LORA ** -0.5),
        'b_kv_norm': gain(ks[20], (DEPTH, B_KV_LORA)),
        'b_w_ukv': nrm(ks[21], (DEPTH, B_KV_LORA, B_HEADS * (B_NOPE + B_V)), B_KV_LORA ** -0.5),
        'c_q_norm': gain(ks[22], (DEPTH, HEAD_DIM)),
        'c_k_norm': gain(ks[23], (DEPTH, HEAD_DIM)),
        'd_lq1': nrm(ks[24], (DEPTH, D_QK), 0.1),
        'd_lk1': nrm(ks[25], (DEPTH, D_QK), 0.1),
        'd_lq2': nrm(ks[26], (DEPTH, D_QK), 0.1),
        'd_lk2': nrm(ks[27], (DEPTH, D_QK), 0.1),
        'd_subln': gain(ks[28], (DEPTH, D_V)),
        'final_norm_g': gain(ks[29], (D_MODEL,)),
    }


def reference(x_prompt, x_sample, cache_a_k, cache_a_v, cache_b_ckv, cache_b_kpe, cache_c_k, cache_c_v,
              cache_d_k, cache_d_v, c, c_ctx, norm_g, w_ada, b_ada, w_in, w_out, a_rpb, b_q_norm, b_w_uq,
              b_kv_norm, b_w_ukv, c_q_norm, c_k_norm, d_lq1, d_lk1, d_lq2, d_lk2, d_subln, final_norm_g):
    def layer_params(l):
        return dict(w_in=w_in[l], w_out=w_out[l], a_rpb=a_rpb[l], b_q_norm=b_q_norm[l], b_w_uq=b_w_uq[l],
                    b_kv_norm=b_kv_norm[l], b_w_ukv=b_w_ukv[l], c_q_norm=c_q_norm[l], c_k_norm=c_k_norm[l],
                    d_lq1=d_lq1[l], d_lk1=d_lk1[l], d_lq2=d_lq2[l], d_lk2=d_lk2[l], d_subln=d_subln[l])

    xp = x_prompt
    ctx_tensors = []
    silu_ctx = jax.nn.silu(c_ctx)
    for l in range(DEPTH):
        lp = layer_params(l)
        lam_init = 0.8 - 0.6 * math.exp(-0.3 * l)
        mod = jnp.einsum('d,de->e', silu_ctx, w_ada[l]) + b_ada[l]
        h, gate = modulate(xp, norm_g[l], mod)
        out, kvs = mix_context(h, lp, lam_init)
        xp = xp + gate * out
        ctx_tensors.append(kvs)
    y_prompt = rmsnorm(xp, final_norm_g)

    def stack_state(i):
        return jnp.stack([t[i] for t in ctx_tensors], axis=1)

    new_a_k = stack_state(0)
    new_a_v = stack_state(1)
    new_b_ckv = stack_state(2)
    new_b_kpe = stack_state(3)
    new_c_k = stack_state(4)
    new_c_v = stack_state(5)
    new_d_k = stack_state(6)
    new_d_v = stack_state(7)

    xs = x_sample
    silu_c = jax.nn.silu(c)
    for l in range(DEPTH):
        lp = layer_params(l)
        lam_init = 0.8 - 0.6 * math.exp(-0.3 * l)
        mod = (jnp.einsum('bd,de->be', silu_c, w_ada[l]) + b_ada[l])[:, None, :]
        h, gate = modulate(xs, norm_g[l], mod)
        ctx = (cache_a_k[:, l], cache_a_v[:, l], cache_b_ckv[:, l], cache_b_kpe[:, l],
               cache_c_k[:, l], cache_c_v[:, l], cache_d_k[:, l], cache_d_v[:, l])
        xs = xs + gate * mix_latent(h, lp, ctx, lam_init)
    y_sample = rmsnorm(xs, final_norm_g)

    return (y_prompt, y_sample, new_a_k, new_a_v, new_b_ckv, new_b_kpe, new_c_k, new_c_v, new_d_k, new_d_v)
```

```python
import functools
import math

import numpy as np
import jax
import jax.numpy as jnp
from jax import lax
from jax.experimental import pallas as pl
from jax.experimental.pallas import tpu as pltpu

F32 = jnp.float32
BF16 = jnp.bfloat16

D_MODEL = 4096
DEPTH = 2
GRID_W = 64
HEAD_DIM = 128
N_HEADS = 8
GROUP_W = N_HEADS * HEAD_DIM
A_KH = 8
A_KW = 16
B_NOPE = 128
B_ROPE = 64
B_Q_LORA = 768
B_KV_LORA = 512
C_KV_HEADS = 2
D_QK = 64
ROPE_THETA = 10000.0
EPS = 1e-6
IN_SIZES = (1024, 1024, 1024, 1024, 768, 512, 64, 1024,
            1024, 256, 256, 1024, 1024, 1024, 1024, 1024)
SEG_NAMES = ("a_q", "a_k", "a_v", "a_z", "b_cq", "b_ckv", "b_kpe", "b_z",
             "c_q", "c_k", "c_v", "c_z", "d_q", "d_k", "d_v", "d_z")

LANE = 128
MASK_NEG = -1e30

SC_A = HEAD_DIM ** -0.5
SC_B = (B_NOPE + B_ROPE) ** -0.5
SC_D = D_QK ** -0.5

TM = 1024
TN = 512
N_Q_TILES = 6
N_KV_TILES = 9
N_GATE_TILES = 8
N_MAIN_TILES = N_Q_TILES + N_KV_TILES + N_GATE_TILES
Q_CHUNK = 256
NORM_TM = 256


def _cparams(n_axes, vmem_mib):
    return pltpu.CompilerParams(dimension_semantics=("arbitrary",) * n_axes,
                                vmem_limit_bytes=vmem_mib * 2 ** 20)


def _rms(z, g):
    ms = jnp.mean(z * z, axis=-1, keepdims=True)
    return z * lax.rsqrt(ms + EPS) * g


def _silu(z):
    return z * jax.nn.sigmoid(z)


def _rope(z, cos, sin_lo, sin_hi, half):
    return (z * cos + pltpu.roll(z, LANE - half, 1) * sin_lo
            + pltpu.roll(z, half, 1) * sin_hi)


def _dot(a, b):
    return jnp.dot(a, b, preferred_element_type=F32)


def _dot_nt(a, b):
    return lax.dot_general(a, b, (((1,), (1,)), ((), ())), preferred_element_type=F32)


def _softmax_pv(q, ks, vs):
    ss = [_dot_nt(q, k) for k in ks]
    m = ss[0].max(axis=-1, keepdims=True)
    for s in ss[1:]:
        m = jnp.maximum(m, s.max(axis=-1, keepdims=True))
    ps = [jnp.exp(s - m) for s in ss]
    l = ps[0].sum(axis=-1, keepdims=True)
    for p in ps[1:]:
        l = l + p.sum(axis=-1, keepdims=True)
    o = _dot(ps[0].astype(BF16), vs[0])
    for p, v in zip(ps[1:], vs[1:]):
        o = o + _dot(p.astype(BF16), v)
    return o / l


def _mods_kernel(c_ref, w_ref, b_ref, o_ref):
    x = _silu(c_ref[...])
    x_hi = x.astype(BF16).astype(F32)
    xs = jnp.concatenate([x_hi, x - x_hi], axis=0).astype(BF16)
    w = w_ref[...]
    w_hi = w.astype(BF16)
    w_lo = (w - w_hi.astype(F32)).astype(BF16)
    r_hi = _dot(xs, w_hi)
    r_lo = _dot(xs, w_lo)
    o_ref[...] = r_hi[:8] + r_hi[8:] + r_lo[:8] + b_ref[...]


def _mods(c8, w_ada, b_ada):
    tn = 512
    n = 3 * D_MODEL
    return pl.pallas_call(
        _mods_kernel,
        out_shape=jax.ShapeDtypeStruct((DEPTH, 8, n), F32),
        grid=(DEPTH, n // tn),
        in_specs=[
            pl.BlockSpec((8, D_MODEL), lambda l, j: (0, 0)),
            pl.BlockSpec((None, D_MODEL, tn), lambda l, j: (l, 0, j)),
            pl.BlockSpec((None, 1, tn), lambda l, j: (l, 0, j)),
        ],
        out_specs=pl.BlockSpec((None, 8, tn), lambda l, j: (l, 0, j)),
        compiler_params=_cparams(2, 48),
        name="mods",
    )(c8, w_ada, b_ada.reshape(DEPTH, 1, n))


def _norm_kernel(*refs, modulate):
    if modulate:
        x_ref, g_ref, mod_ref, o_ref = refs
    else:
        x_ref, g_ref, o_ref = refs
    y = _rms(x_ref[...], g_ref[...])
    if modulate:
        shift = mod_ref[:, 0:D_MODEL]
        scale = mod_ref[:, D_MODEL:2 * D_MODEL]
        y = y * (1.0 + scale) + shift
    o_ref[...] = y.astype(o_ref.dtype)


def _norm(x, g, mods4=None, layer=0, row_fn=None, out_dtype=BF16):
    m = x.shape[0]
    modulate = mods4 is not None
    in_specs = [pl.BlockSpec((NORM_TM, D_MODEL), lambda i: (i, 0)),
                pl.BlockSpec((1, D_MODEL), lambda i: (0, 0))]
    args = [x, g]
    if modulate:
        in_specs.append(pl.BlockSpec((None, None, 1, 3 * D_MODEL),
                                     lambda i: (layer, row_fn(i * NORM_TM), 0, 0)))
        args.append(mods4)
    return pl.pallas_call(
        functools.partial(_norm_kernel, modulate=modulate),
        out_shape=jax.ShapeDtypeStruct((m, D_MODEL), out_dtype),
        grid=(m // NORM_TM,),
        in_specs=in_specs,
        out_specs=pl.BlockSpec((NORM_TM, D_MODEL), lambda i: (i, 0)),
        compiler_params=_cparams(1, 32),
        name="norm_mod" if modulate else "norm_final",
    )(*args)


def _main_proj_kernel(*refs, lat):
    if lat:
        (h_ref, w_ref, cqg_ref, ckg_ref, cosc, sloc, shic, cosd, slod, shid,
         zb_ref, g_ref, acc_ref) = refs
        q_out = kv_out = zb_ref
        kv_dt = BF16
    else:
        h_ref, w_ref, cqg_ref, ckg_ref, zq_ref, zf_ref, g_ref, acc_ref = refs
        q_out, kv_out = zq_ref, zf_ref
        kv_dt = F32
    j = pl.program_id(1)
    acc_ref[...] = _dot(h_ref[...], w_ref[...])

    def rope_c(z):
        return _rope(z, cosc[...], sloc[...], shic[...], 32) if lat else z

    def rope_d(z):
        return _rope(z, cosd[...], slod[...], shid[...], 16) if lat else z

    def emit(out_ref, fn, dt):
        for k in range(TN // LANE):
            sl = slice(k * LANE, (k + 1) * LANE)
            out_ref[:, sl] = fn(acc_ref[:, sl], k).astype(dt)

    @pl.when(j < 2)
    def _():
        emit(q_out, lambda z, k: z * SC_A, BF16)

    @pl.when((j >= 2) & (j < 4))
    def _():
        emit(q_out, lambda z, k: rope_c(_rms(z, cqg_ref[...])) * SC_A, BF16)

    @pl.when((j >= 4) & (j < 6))
    def _():
        emit(q_out, lambda z, k: rope_d(z) * SC_D, BF16)

    @pl.when(((j >= 6) & (j < 10)) | ((j >= 12) & (j < 14)))
    def _():
        emit(kv_out, lambda z, k: z, kv_dt)

    @pl.when((j >= 10) & (j < 12))
    def _():
        emit(kv_out, lambda z, k: rope_d(z), kv_dt)

    @pl.when(j == 14)
    def _():
        emit(kv_out, lambda z, k: rope_c(_rms(z, ckg_ref[...])) if k < 2 else z, kv_dt)

    @pl.when(j >= N_Q_TILES + N_KV_TILES)
    def _():
        emit(g_ref, lambda z, k: _silu(z), F32)


def _main_proj(h, w_main, cq_gain, ck_gain, tables, lat):
    m = h.shape[0]
    n_bf = N_Q_TILES + N_KV_TILES
    in_specs = [
        pl.BlockSpec((TM, D_MODEL), lambda i, j: (i, 0)),
        pl.BlockSpec((D_MODEL, TN), lambda i, j: (0, j)),
        pl.BlockSpec((1, LANE), lambda i, j: (0, 0)),
        pl.BlockSpec((1, LANE), lambda i, j: (0, 0)),
    ]
    args = [h, w_main, cq_gain, ck_gain]
    g_spec = pl.BlockSpec((TM, TN), lambda i, j: (i, jnp.maximum(j - n_bf, 0)))
    g_shape = jax.ShapeDtypeStruct((m, N_GATE_TILES * TN), F32)
    if lat:
        t_blocks = tables[0].shape[0] // TM
        for t in tables:
            in_specs.append(pl.BlockSpec((TM, LANE), lambda i, j: (i % t_blocks, 0)))
            args.append(t)
        out_shape = (jax.ShapeDtypeStruct((m, n_bf * TN), BF16), g_shape)
        out_specs = (pl.BlockSpec((TM, TN), lambda i, j: (i, jnp.minimum(j, n_bf - 1))), g_spec)
    else:
        out_shape = (jax.ShapeDtypeStruct((m, N_Q_TILES * TN), BF16),
                     jax.ShapeDtypeStruct((m, N_KV_TILES * TN), F32), g_shape)
        out_specs = (pl.BlockSpec((TM, TN), lambda i, j: (i, jnp.minimum(j, N_Q_TILES - 1))),
                     pl.BlockSpec((TM, TN), lambda i, j: (i, jnp.clip(j - N_Q_TILES, 0, N_KV_TILES - 1))),
                     g_spec)
    return pl.pallas_call(
        functools.partial(_main_proj_kernel, lat=lat),
        out_shape=out_shape,
        grid=(m // TM, N_MAIN_TILES),
        in_specs=in_specs,
        out_specs=out_specs,
        scratch_shapes=[pltpu.VMEM((TM, TN), F32)],
        compiler_params=_cparams(2, 56),
        name="main_proj_lat" if lat else "main_proj_ctx",
    )(*args)


def _bq_kernel(*refs, lat):
    if lat:
        h_ref, w_ref, g_ref, wuq_ref, cosd, slod, shid, o_ref = refs
    else:
        h_ref, w_ref, g_ref, wuq_ref, o_ref = refs
    cq = _rms(_dot(h_ref[...], w_ref[...]), g_ref[...]).astype(BF16)
    q = _dot(cq, wuq_ref[...])
    for hh in range(N_HEADS):
        c0 = hh * 2 * LANE
        o_ref[:, c0:c0 + LANE] = (q[:, c0:c0 + LANE] * SC_B).astype(BF16)
        pe = q[:, c0 + LANE:c0 + 2 * LANE]
        if lat:
            pe = _rope(pe, cosd[...], slod[...], shid[...], 16)
        o_ref[:, c0 + LANE:c0 + 2 * LANE] = (pe * SC_B).astype(BF16)


def _bq_proj(h, w_bcq, gain, w_uq, tables_d, lat):
    m = h.shape[0]
    in_specs = [
        pl.BlockSpec((TM, D_MODEL), lambda i: (i, 0)),
        pl.BlockSpec((D_MODEL, B_Q_LORA), lambda i: (0, 0)),
        pl.BlockSpec((1, B_Q_LORA), lambda i: (0, 0)),
        pl.BlockSpec((B_Q_LORA, N_HEADS * 2 * LANE), lambda i: (0, 0)),
    ]
    args = [h, w_bcq, gain, w_uq]
    if lat:
        t_blocks = tables_d[0].shape[0] // TM
        for t in tables_d:
            in_specs.append(pl.BlockSpec((TM, LANE), lambda i: (i % t_blocks, 0)))
            args.append(t)
    return pl.pallas_call(
        functools.partial(_bq_kernel, lat=lat),
        out_shape=jax.ShapeDtypeStruct((m, N_HEADS * 2 * LANE), BF16),
        grid=(m // TM,),
        in_specs=in_specs,
        out_specs=pl.BlockSpec((TM, N_HEADS * 2 * LANE), lambda i: (i, 0)),
        compiler_params=_cparams(1, 56),
        name="bq_proj_lat" if lat else "bq_proj_ctx",
    )(*args)


def _expand_kv(ckv, kpe, wukv_ref, kb_ref, vb_ref):
    kv = _dot(ckv.astype(BF16), wukv_ref[...])
    kpe_b = kpe.astype(BF16)
    for hh in range(N_HEADS):
        c0 = hh * 2 * LANE
        kb_ref[:, c0:c0 + LANE] = kv[:, c0:c0 + LANE].astype(BF16)
        kb_ref[:, c0 + LANE:c0 + 2 * LANE] = kpe_b
        vb_ref[:, hh * LANE:(hh + 1) * LANE] = kv[:, c0 + LANE:c0 + 2 * LANE].astype(BF16)


def _bkv_kernel(*refs, lat):
    if lat:
        h_ref, w_ref, g_ref, wukv_ref, cosd, slod, shid, kb_ref, vb_ref = refs
    else:
        h_ref, w_ref, g_ref, wukv_ref, kb_ref, vb_ref, ckv_ref, kpe_ref = refs
    z = _dot(h_ref[...], w_ref[...])
    ckv = _rms(z[:, :B_KV_LORA], g_ref[...])
    kpe = z[:, B_KV_LORA:B_KV_LORA + LANE]
    if lat:
        kpe = _rope(kpe, cosd[...], slod[...], shid[...], 16)
    else:
        ckv_ref[...] = ckv
        kpe_ref[...] = kpe[:, :B_ROPE]
    _expand_kv(ckv, kpe, wukv_ref, kb_ref, vb_ref)


def _bkv_proj(h, w_bkv, gain, w_ukv, tables_d, lat):
    m = h.shape[0]
    wcols = B_KV_LORA + LANE
    in_specs = [
        pl.BlockSpec((TM, D_MODEL), lambda i: (i, 0)),
        pl.BlockSpec((D_MODEL, wcols), lambda i: (0, 0)),
        pl.BlockSpec((1, B_KV_LORA), lambda i: (0, 0)),
        pl.BlockSpec((B_KV_LORA, N_HEADS * 2 * LANE), lambda i: (0, 0)),
    ]
    args = [h, w_bkv, gain, w_ukv]
    out_shape = [jax.ShapeDtypeStruct((m, N_HEADS * 2 * LANE), BF16),
                 jax.ShapeDtypeStruct((m, GROUP_W), BF16)]
    out_specs = [pl.BlockSpec((TM, N_HEADS * 2 * LANE), lambda i: (i, 0)),
                 pl.BlockSpec((TM, GROUP_W), lambda i: (i, 0))]
    if lat:
        t_blocks = tables_d[0].shape[0] // TM
        for t in tables_d:
            in_specs.append(pl.BlockSpec((TM, LANE), lambda i: (i % t_blocks, 0)))
            args.append(t)
    else:
        out_shape += [jax.ShapeDtypeStruct((m, B_KV_LORA), F32),
                      jax.ShapeDtypeStruct((m, B_ROPE), F32)]
        out_specs += [pl.BlockSpec((TM, B_KV_LORA), lambda i: (i, 0)),
                      pl.BlockSpec((TM, B_ROPE), lambda i: (i, 0))]
    return pl.pallas_call(
        functools.partial(_bkv_kernel, lat=lat),
        out_shape=tuple(out_shape),
        grid=(m // TM,),
        in_specs=in_specs,
        out_specs=tuple(out_specs),
        compiler_params=_cparams(1, 56),
        name="bkv_proj_lat" if lat else "bkv_proj_ctx",
    )(*args)


def _cache_kv_kernel(ckv_ref, kpe_ref, wukv_ref, kb_ref, vb_ref):
    _expand_kv(ckv_ref[...], kpe_ref[...], wukv_ref, kb_ref, vb_ref)


def _cache_kv(cache_ckv, cache_kpe128, w_ukv, layer):
    nb, _, s, _ = cache_ckv.shape
    return pl.pallas_call(
        _cache_kv_kernel,
        out_shape=(jax.ShapeDtypeStruct((nb * s, N_HEADS * 2 * LANE), BF16),
                   jax.ShapeDtypeStruct((nb * s, GROUP_W), BF16)),
        grid=(nb,),
        in_specs=[
            pl.BlockSpec((None, None, s, B_KV_LORA), lambda b: (b, layer, 0, 0)),
            pl.BlockSpec((None, None, s, LANE), lambda b: (b, layer, 0, 0)),
            pl.BlockSpec((B_KV_LORA, N_HEADS * 2 * LANE), lambda b: (0, 0)),
        ],
        out_specs=(pl.BlockSpec((s, N_HEADS * 2 * LANE), lambda b: (b, 0)),
                   pl.BlockSpec((s, GROUP_W), lambda b: (b, 0))),
        compiler_params=_cparams(1, 32),
        name="cache_kv",
    )(cache_ckv, cache_kpe128, w_ukv)


def _attn_kernel(*refs, n_src, nh, group, dk, n_chunks, diff, lam_init):
    q_ref = refs[0]
    kv_refs = refs[1:1 + 2 * n_src]
    g_ref = refs[1 + 2 * n_src]
    if diff:
        lq1, lk1, lq2, lk2, subln = refs[2 + 2 * n_src:7 + 2 * n_src]
        lam = (jnp.exp(jnp.sum(lq1[...] * lk1[...], axis=-1, keepdims=True))
               - jnp.exp(jnp.sum(lq2[...] * lk2[...], axis=-1, keepdims=True)) + lam_init)
        lane = lax.broadcasted_iota(jnp.int32, (1, LANE), 1)
        m_lo = jnp.where(lane < D_QK, 1.0, 0.0).astype(BF16)
        m_hi = jnp.where(lane >= D_QK, 1.0, 0.0).astype(BF16)
    o_ref = refs[-1]

    def chunk(c):
        if isinstance(c, int):
            rows = slice(c * Q_CHUNK, (c + 1) * Q_CHUNK)
        else:
            rows = pl.ds(pl.multiple_of(c * Q_CHUNK, Q_CHUNK), Q_CHUNK)
        for hh in range(nh):
            kh = hh // group
            ks = [kv_refs[2 * s][:, kh * dk:(kh + 1) * dk].astype(BF16) for s in range(n_src)]
            vs = [kv_refs[2 * s + 1][:, kh * LANE:(kh + 1) * LANE].astype(BF16) for s in range(n_src)]
            q = q_ref[rows, hh * dk:(hh + 1) * dk]
            osl = slice(hh * LANE, (hh + 1) * LANE)
            if diff:
                o1 = _softmax_pv(q * m_lo, ks, vs)
                o2 = _softmax_pv(q * m_hi, ks, vs)
                o = _rms(o1 - lam * o2, subln[...]) * (1.0 - lam_init)
            else:
                o = _softmax_pv(q, ks, vs)
            o_ref[rows, osl] = (o * g_ref[rows, osl]).astype(BF16)

    if n_chunks == 1:
        chunk(0)
    else:
        def body(c, carry):
            chunk(c)
            return carry
        lax.fori_loop(0, n_chunks, body, 0)


def _attn(q, q_spec, srcs, g, g_spec, out_rows, out_spec, grid, *, nh, group, dk, t_q,
          diff_params=None, lam_init=0.0, name):
    in_specs = [q_spec]
    args = [q]
    for ka, kspec, va, vspec in srcs:
        in_specs += [kspec, vspec]
        args += [ka, va]
    in_specs.append(g_spec)
    args.append(g)
    diff = diff_params is not None
    if diff:
        for p in diff_params:
            in_specs.append(pl.BlockSpec(p.shape, lambda *idx: (0, 0)))
            args.append(p)
    return pl.pallas_call(
        functools.partial(_attn_kernel, n_src=len(srcs), nh=nh, group=group, dk=dk,
                          n_chunks=t_q // Q_CHUNK, diff=diff, lam_init=lam_init),
        out_shape=jax.ShapeDtypeStruct((out_rows, GROUP_W), BF16),
        grid=grid,
        in_specs=in_specs,
        out_specs=out_spec,
        compiler_params=_cparams(len(grid), 48),
        name=name,
    )(*args)


NAT_ROWS_PER_CHUNK = Q_CHUNK // GRID_W
NAT_WIN_ROWS = 12
NAT_N_DR = 2 * A_KH - 1
NAT_N_DC = 2 * A_KW - 1


def _natten_kernel(rpb_ref, q_ref, k_ref, v_ref, kc_ref, vc_ref, g_ref, o_ref,
                   tl_ref, tr_ref, *, n_rows):
    h = pl.program_id(0)
    b = pl.program_id(1)

    @pl.when(b == 0)
    def _build():
        cq = lax.broadcasted_iota(jnp.int32, (GRID_W, LANE), 0)
        lane = lax.broadcasted_iota(jnp.int32, (GRID_W, LANE), 1)
        left = lane < GRID_W
        ck = jnp.where(left, lane, lane - GRID_W)
        delta = jnp.clip(ck - cq, -(A_KW - 1), A_KW - 1) + (A_KW - 1)
        start_c = jnp.clip(cq - A_KW // 2, 0, GRID_W - A_KW)
        col_ok = (ck >= start_c) & (ck < start_c + A_KW)
        for d in range(NAT_N_DR):
            t = jnp.zeros((GRID_W, LANE), F32)
            for e in range(NAT_N_DC):
                t = jnp.where(delta == e, rpb_ref[h * (NAT_N_DR * NAT_N_DC) + d * NAT_N_DC + e], t)
            t = jnp.where(col_ok, t, MASK_NEG)
            tl_ref[d] = jnp.where(left, t, 0.0)
            tr_ref[d] = jnp.where(left, 0.0, t)
        tl_ref[NAT_N_DR] = jnp.where(left, MASK_NEG, 0.0)
        tr_ref[NAT_N_DR] = jnp.where(left, 0.0, MASK_NEG)

    kc = kc_ref[...].astype(BF16)
    vc = vc_ref[...].astype(BF16)
    kh = min(A_KH, n_rows)

    def body(c, carry):
        r0 = c * NAT_ROWS_PER_CHUNK
        ws = jnp.clip(r0 - A_KH // 2, 0, n_rows - NAT_WIN_ROWS)
        krows = pl.ds(pl.multiple_of(ws * GRID_W, GRID_W), NAT_WIN_ROWS * GRID_W)
        rows = pl.ds(pl.multiple_of(c * Q_CHUNK, Q_CHUNK), Q_CHUNK)
        q = q_ref[rows, :]
        kw = k_ref[krows, :]
        vw = v_ref[krows, :]
        bias_rows = []
        for qi in range(NAT_ROWS_PER_CHUNK):
            rq = r0 + qi
            st = jnp.clip(rq - kh // 2, 0, n_rows - kh)
            blks = []
            for jj in range(NAT_WIN_ROWS // 2):
                idx = []
                for half in range(2):
                    rk = ws + 2 * jj + half
                    valid = (rk >= st) & (rk < st + kh)
                    idx.append(jnp.where(valid, rk - rq + (A_KH - 1), NAT_N_DR))
                blks.append(tl_ref[idx[0]] + tr_ref[idx[1]])
            bias_rows.append(jnp.concatenate(blks, axis=1))
        bias = jnp.concatenate(bias_rows, axis=0)
        s_loc = _dot_nt(q, kw) + bias
        s_ctx = _dot_nt(q, kc)
        m = jnp.maximum(s_loc.max(axis=-1, keepdims=True), s_ctx.max(axis=-1, keepdims=True))
        p_loc = jnp.exp(s_loc - m)
        p_ctx = jnp.exp(s_ctx - m)
        l = p_loc.sum(axis=-1, keepdims=True) + p_ctx.sum(axis=-1, keepdims=True)
        o = (_dot(p_loc.astype(BF16), vw) + _dot(p_ctx.astype(BF16), vc)) / l
        o_ref[rows, :] = (o * g_ref[rows, :]).astype(BF16)
        return carry

    lax.fori_loop(0, q_ref.shape[0] // Q_CHUNK, body, 0)


def _natten(rpb_flat, zb, cache_k, cache_v, g, layer, nb, t):
    s_ctx = cache_k.shape[2]
    n_rows = t // GRID_W
    return pl.pallas_call(
        functools.partial(_natten_kernel, n_rows=n_rows),
        out_shape=jax.ShapeDtypeStruct((nb * t, GROUP_W), BF16),
        grid=(N_HEADS, nb),
        in_specs=[
            pl.BlockSpec(memory_space=pltpu.SMEM),
            pl.BlockSpec((t, LANE), lambda h, b: (b, h)),
            pl.BlockSpec((t, LANE), lambda h, b: (b, 24 + h)),
            pl.BlockSpec((t, LANE), lambda h, b: (b, 32 + h)),
            pl.BlockSpec((None, None, s_ctx, LANE), lambda h, b: (b, layer, 0, h)),
            pl.BlockSpec((None, None, s_ctx, LANE), lambda h, b: (b, layer, 0, h)),
            pl.BlockSpec((t, LANE), lambda h, b: (b, h)),
        ],
        out_specs=pl.BlockSpec((t, LANE), lambda h, b: (b, h)),
        scratch_shapes=[pltpu.VMEM((NAT_N_DR + 1, GRID_W, LANE), F32),
                        pltpu.VMEM((NAT_N_DR + 1, GRID_W, LANE), F32)],
        compiler_params=_cparams(2, 48),
        name="natten",
    )(rpb_flat, zb, zb, zb, cache_k, cache_v, g)


def _outproj_kernel(ya, yb, yc, yd, w_ref, x_ref, gate_ref, o_ref):
    acc = _dot(ya[...], w_ref[0:GROUP_W, :])
    acc = acc + _dot(yb[...], w_ref[GROUP_W:2 * GROUP_W, :])
    acc = acc + _dot(yc[...], w_ref[2 * GROUP_W:3 * GROUP_W, :])
    acc = acc + _dot(yd[...], w_ref[3 * GROUP_W:4 * GROUP_W, :])
    o_ref[...] = x_ref[...] + gate_ref[...] * acc


def _outproj(ys, w_out, x, mods4, layer, row_fn):
    m = x.shape[0]
    gate_blk0 = 2 * D_MODEL // TN
    y_spec = pl.BlockSpec((TM, GROUP_W), lambda i, j: (i, 0))
    return pl.pallas_call(
        _outproj_kernel,
        out_shape=jax.ShapeDtypeStruct((m, D_MODEL), F32),
        grid=(m // TM, D_MODEL // TN),
        in_specs=[y_spec, y_spec, y_spec, y_spec,
                  pl.BlockSpec((D_MODEL, TN), lambda i, j: (0, j)),
                  pl.BlockSpec((TM, TN), lambda i, j: (i, j)),
                  pl.BlockSpec((None, None, 1, TN),
                               lambda i, j: (layer, row_fn(i * TM), 0, gate_blk0 + j))],
        out_specs=pl.BlockSpec((TM, TN), lambda i, j: (i, j)),
        compiler_params=_cparams(2, 48),
        name="outproj",
    )(*ys, w_out, x, mods4)


def _axial_tables(t, head_dim):
    pos = np.arange(t)
    part = head_dim // 2
    half = part // 2
    freqs = ROPE_THETA ** (-(np.arange(half, dtype=np.float64) * 2.0 / part))
    lane = np.arange(LANE) % head_dim
    p = np.where(lane < part, (pos // GRID_W)[:, None], (pos % GRID_W)[:, None])
    within = lane % part
    ang = p * freqs[within % half][None, :]
    first = (within < half)[None, :]
    cos = np.cos(ang)
    sin = np.sin(ang)
    return (jnp.asarray(cos, F32), jnp.asarray(np.where(first, -sin, 0.0), F32),
            jnp.asarray(np.where(first, 0.0, sin), F32))


def _prep_layer(w_in_l, w_out_l, b_w_uq_l, b_w_ukv_l):
    offs = np.concatenate([[0], np.cumsum(IN_SIZES)])
    seg = {n: w_in_l[:, offs[i]:offs[i + 1]] for i, n in enumerate(SEG_NAMES)}
    order = ("a_q", "c_q", "d_q", "a_k", "a_v", "d_k", "d_v", "c_k", "c_v",
             "a_z", "b_z", "c_z", "d_z")
    w_main = jnp.concatenate([seg[n] for n in order], axis=1).astype(BF16)
    w_bcq = seg["b_cq"].astype(BF16)
    w_bkv = jnp.concatenate(
        [seg["b_ckv"], seg["b_kpe"], jnp.zeros((D_MODEL, LANE - B_ROPE), F32)], axis=1).astype(BF16)
    w_uq = jnp.pad(b_w_uq_l.reshape(B_Q_LORA, N_HEADS, B_NOPE + B_ROPE),
                   ((0, 0), (0, 0), (0, 2 * LANE - B_NOPE - B_ROPE)))
    w_uq = w_uq.reshape(B_Q_LORA, N_HEADS * 2 * LANE).astype(BF16)
    return dict(w_main=w_main, w_bcq=w_bcq, w_bkv=w_bkv, w_uq=w_uq,
                w_ukv=b_w_ukv_l.astype(BF16), w_out=w_out_l.astype(BF16))


def _ctx_layer(x, l, lam_init, wl, p, mods4, nb, t):
    row_fn = lambda r: 0
    h = _norm(x, p["norm_g"], mods4, l, row_fn)
    zq, zf, g = _main_proj(h, wl["w_main"], p["c_q_norm"], p["c_k_norm"], None, lat=False)
    qb = _bq_proj(h, wl["w_bcq"], p["b_q_norm"], wl["w_uq"], None, lat=False)
    kb, vb, ckv_state, kpe_state = _bkv_proj(h, wl["w_bkv"], p["b_kv_norm"], wl["w_ukv"], None, lat=False)

    grid = (nb,)
    wide = lambda blk: pl.BlockSpec((t, GROUP_W), lambda b: (b, blk))
    out_spec = pl.BlockSpec((t, GROUP_W), lambda b: (b, 0))
    m = nb * t
    ya = _attn(zq, wide(0), [(zf, wide(0), zf, wide(1))], g, wide(0), m, out_spec, grid,
               nh=N_HEADS, group=1, dk=LANE, t_q=t, name="attn_a_ctx")
    yb = _attn(qb, pl.BlockSpec((t, 2 * GROUP_W), lambda b: (b, 0)),
               [(kb, pl.BlockSpec((t, 2 * GROUP_W), lambda b: (b, 0)), vb, wide(0))],
               g, wide(1), m, out_spec, grid, nh=N_HEADS, group=1, dk=2 * LANE, t_q=t,
               name="attn_b_ctx")
    ckv_w = C_KV_HEADS * HEAD_DIM
    c_k_blk = 4 * GROUP_W // ckv_w
    yc = _attn(zq, wide(1),
               [(zf, pl.BlockSpec((t, ckv_w), lambda b: (b, c_k_blk)),
                 zf, pl.BlockSpec((t, ckv_w), lambda b: (b, c_k_blk + 1)))],
               g, wide(2), m, out_spec, grid, nh=N_HEADS, group=N_HEADS // C_KV_HEADS,
               dk=LANE, t_q=t, name="attn_c_ctx")
    yd = _attn(zq, wide(2), [(zf, wide(2), zf, wide(3))], g, wide(3), m, out_spec, grid,
               nh=N_HEADS, group=1, dk=LANE, t_q=t,
               diff_params=p["diff"], lam_init=lam_init, name="attn_d_ctx")
    x_new = _outproj((ya, yb, yc, yd), wl["w_out"], x, mods4, l, row_fn)
    states = dict(
        a_k=zf[:, 0:GROUP_W], a_v=zf[:, GROUP_W:2 * GROUP_W],
        d_k=zf[:, 2 * GROUP_W:3 * GROUP_W], d_v=zf[:, 3 * GROUP_W:4 * GROUP_W],
        c_k=zf[:, 4 * GROUP_W:4 * GROUP_W + ckv_w], c_v=zf[:, 4 * GROUP_W + ckv_w:4 * GROUP_W + 2 * ckv_w],
        b_ckv=ckv_state, b_kpe=kpe_state)
    return x_new, states


def _lat_layer(x, l, lam_init, wl, p, mods4, caches, tables_c, tables_d, nb, t):
    row_fn = lambda r: 1 + r // t
    h = _norm(x, p["norm_g"], mods4, l, row_fn)
    zb, g = _main_proj(h, wl["w_main"], p["c_q_norm"], p["c_k_norm"], tables_c + tables_d, lat=True)
    qb = _bq_proj(h, wl["w_bcq"], p["b_q_norm"], wl["w_uq"], tables_d, lat=True)
    kb, vb = _bkv_proj(h, wl["w_bkv"], p["b_kv_norm"], wl["w_ukv"], tables_d, lat=True)
    kbc, vbc = _cache_kv(caches["b_ckv"], caches["b_kpe"], wl["w_ukv"], l)
    s_ctx = caches["a_k"].shape[2]
    m = nb * t

    ya = _natten(p["rpb"], zb, caches["a_k"], caches["a_v"], g, l, nb, t)

    grid = (nb, N_HEADS)
    col = lambda blk0, div=1: pl.BlockSpec((t, LANE), lambda b, h: (b, blk0 + h // div))
    cache = lambda div=1: pl.BlockSpec((None, None, s_ctx, LANE), lambda b, h: (b, l, 0, h // div))
    out_spec = pl.BlockSpec((t, LANE), lambda b, h: (b, h))
    wide2 = lambda rows: pl.BlockSpec((rows, 2 * LANE), lambda b, h: (b, h))
    yb = _attn(qb, wide2(t),
               [(kb, wide2(t), vb, col(0)),
                (kbc, wide2(s_ctx), vbc, pl.BlockSpec((s_ctx, LANE), lambda b, h: (b, h)))],
               g, col(8), m, out_spec, grid, nh=1, group=1, dk=2 * LANE, t_q=t, name="attn_b_lat")
    grp = N_HEADS // C_KV_HEADS
    yc = _attn(zb, col(8),
               [(zb, col(56, grp), zb, col(58, grp)),
                (caches["c_k"], cache(grp), caches["c_v"], cache(grp))],
               g, col(16), m, out_spec, grid, nh=1, group=1, dk=LANE, t_q=t, name="attn_c_lat")
    yd = _attn(zb, col(16),
               [(zb, col(40), zb, col(48)),
                (caches["d_k"], cache(), caches["d_v"], cache())],
               g, col(24), m, out_spec, grid, nh=1, group=1, dk=LANE, t_q=t,
               diff_params=p["diff"], lam_init=lam_init, name="attn_d_lat")
    return _outproj((ya, yb, yc, yd), wl["w_out"], x, mods4, l, row_fn)


def kernel(x_prompt, x_sample, cache_a_k, cache_a_v, cache_b_ckv, cache_b_kpe, cache_c_k, cache_c_v, cache_d_k, cache_d_v, c, c_ctx, norm_g, w_ada, b_ada, w_in, w_out, a_rpb, b_q_norm, b_w_uq, b_kv_norm, b_w_ukv, c_q_norm, c_k_norm, d_lq1, d_lk1, d_lq2, d_lk2, d_subln, final_norm_g):
    nb_c, t_c, _ = x_prompt.shape
    nb_l, t_l, _ = x_sample.shape
    s_ctx = cache_a_k.shape[2]

    c8 = jnp.concatenate([c_ctx[None, :], c, jnp.zeros((8 - 1 - nb_l, D_MODEL), F32)], axis=0)
    mods = _mods(c8, w_ada, b_ada)
    mods4 = mods[:, :1 + nb_l].reshape(DEPTH, 1 + nb_l, 1, 3 * D_MODEL)

    caches = dict(
        a_k=cache_a_k.reshape(nb_l, DEPTH, s_ctx, GROUP_W),
        a_v=cache_a_v.reshape(nb_l, DEPTH, s_ctx, GROUP_W),
        b_ckv=cache_b_ckv,
        b_kpe=jnp.pad(cache_b_kpe, ((0, 0), (0, 0), (0, 0), (0, LANE - B_ROPE))),
        c_k=cache_c_k.reshape(nb_l, DEPTH, s_ctx, C_KV_HEADS * HEAD_DIM),
        c_v=cache_c_v.reshape(nb_l, DEPTH, s_ctx, C_KV_HEADS * HEAD_DIM),
        d_k=cache_d_k.reshape(nb_l, DEPTH, s_ctx, GROUP_W),
        d_v=cache_d_v.reshape(nb_l, DEPTH, s_ctx, GROUP_W),
    )
    tables_c = _axial_tables(t_l, HEAD_DIM)
    tables_d = _axial_tables(t_l, D_QK)

    xp = x_prompt.reshape(nb_c * t_c, D_MODEL)
    xs = x_sample.reshape(nb_l * t_l, D_MODEL)
    ctx_states = []
    for l in range(DEPTH):
        lam_init = 0.8 - 0.6 * math.exp(-0.3 * l)
        wl = _prep_layer(w_in[l], w_out[l], b_w_uq[l], b_w_ukv[l])
        p = dict(
            norm_g=norm_g[l][None, :], c_q_norm=c_q_norm[l][None, :], c_k_norm=c_k_norm[l][None, :],
            b_q_norm=b_q_norm[l][None, :], b_kv_norm=b_kv_norm[l][None, :],
            rpb=a_rpb[l].reshape(-1),
            diff=(d_lq1[l][None, :], d_lk1[l][None, :], d_lq2[l][None, :], d_lk2[l][None, :],
                  d_subln[l][None, :]),
        )
        xp, st = _ctx_layer(xp, l, lam_init, wl, p, mods4, nb_c, t_c)
        ctx_states.append(st)
        xs = _lat_layer(xs, l, lam_init, wl, p, mods4, caches, tables_c, tables_d, nb_l, t_l)

    fg = final_norm_g[None, :]
    y_prompt = _norm(xp, fg, out_dtype=F32).reshape(nb_c, t_c, D_MODEL)
    y_sample = _norm(xs, fg, out_dtype=F32).reshape(nb_l, t_l, D_MODEL)

    def stack(name, tail):
        return jnp.stack([s[name].reshape((nb_c, t_c) + tail) for s in ctx_states], axis=1)

    return (y_prompt, y_sample,
            stack("a_k", (N_HEADS, HEAD_DIM)), stack("a_v", (N_HEADS, HEAD_DIM)),
            stack("b_ckv", (B_KV_LORA,)), stack("b_kpe", (B_ROPE,)),
            stack("c_k", (C_KV_HEADS, HEAD_DIM)), stack("c_v", (C_KV_HEADS, HEAD_DIM)),
            stack("d_k", (N_HEADS, 2, D_QK)), stack("d_v", (N_HEADS, HEAD_DIM)))
```

```python
import functools
import math

import numpy as np
import jax
import jax.numpy as jnp
from jax import lax
from jax.experimental import pallas as pl
from jax.experimental.pallas import tpu as pltpu

F32 = jnp.float32
BF16 = jnp.bfloat16

D_MODEL = 4096
DEPTH = 2
GRID_W = 64
HEAD_DIM = 128
N_HEADS = 8
GROUP_W = N_HEADS * HEAD_DIM
A_KH = 8
A_KW = 16
B_NOPE = 128
B_ROPE = 64
B_Q_LORA = 768
B_KV_LORA = 512
C_KV_HEADS = 2
D_QK = 64
ROPE_THETA = 10000.0
EPS = 1e-6

LANE = 128
MASK_NEG = -1e30

LOG2E = math.log2(math.e)
QS_A = HEAD_DIM ** -0.5 * LOG2E
QS_B = (B_NOPE + B_ROPE) ** -0.5 * LOG2E
QS_D = D_QK ** -0.5 * LOG2E

TM = 1024
TN = 512
TM_B = 512
Q_CHUNK = 256
LAT_HEADS_PER_STEP = 2
NORM_TM = 256


def _cparams(n_axes, vmem_mib):
    return pltpu.CompilerParams(dimension_semantics=("arbitrary",) * n_axes,
                                vmem_limit_bytes=vmem_mib * 2 ** 20)


def _rms(z, g):
    ms = jnp.mean(z * z, axis=-1, keepdims=True)
    return z * lax.rsqrt(ms + EPS) * g


def _silu(z):
    return z * jax.nn.sigmoid(z)


def _rope(z, cos, sin_lo, sin_hi, half):
    return (z * cos + pltpu.roll(z, LANE - half, 1) * sin_lo
            + pltpu.roll(z, half, 1) * sin_hi)


def _dot(a, b):
    return jnp.dot(a, b, preferred_element_type=F32)


def _dot_nt(a, b):
    return lax.dot_general(a, b, (((1,), (1,)), ((), ())), preferred_element_type=F32)


def _stage_qk(q, ks, biases, s_ref, m_ref):
    off = 0
    m = None
    for k, bias in zip(ks, biases):
        st = _dot_nt(k, q)
        if bias is not None:
            st = st + bias
        s_ref[off:off + k.shape[0], :] = st
        mi = st.max(axis=0, keepdims=True)
        m = mi if m is None else jnp.maximum(m, mi)
        off += k.shape[0]
    m_ref[...] = m


def _stage_softmax(s_ref, m_ref, p_ref, l_ref):
    p = jnp.exp2(s_ref[...] - m_ref[...])
    l_ref[...] = p.sum(axis=0, keepdims=True)
    p_ref[...] = p.astype(BF16)


def _stage_pv(vts, p_ref, l_ref):
    off = 0
    o_t = None
    for vt in vts:
        part = _dot(vt, p_ref[off:off + vt.shape[1], :])
        o_t = part if o_t is None else o_t + part
        off += vt.shape[1]
    return o_t / l_ref[...]


def _pipeline(n_chunks, cpc, qk, sm, pv):
    assert cpc % 2 == 0

    def prev(c, j, back):
        return (c, j - back) if j >= back else (c - 1, j - back + cpc)

    for j in range(cpc):
        qk(0, j)
        if j >= 1:
            sm(0, j - 1)
        if j >= 2:
            pv(0, j - 2)
    if n_chunks > 1:
        def body(c, carry):
            for j in range(cpc):
                qk(c, j)
                sm(*prev(c, j, 1))
                pv(*prev(c, j, 2))
            return carry
        lax.fori_loop(1, n_chunks, body, 0)
    last = n_chunks - 1
    sm(last, cpc - 1)
    pv(last, cpc - 2)
    pv(last, cpc - 1)


def _mods_kernel(c_ref, w_ref, b_ref, o_ref):
    x = _silu(c_ref[...])
    x_hi = x.astype(BF16).astype(F32)
    xs = jnp.concatenate([x_hi, x - x_hi], axis=0).astype(BF16)
    w = w_ref[...]
    w_hi = w.astype(BF16)
    w_lo = (w - w_hi.astype(F32)).astype(BF16)
    r_hi = _dot(xs, w_hi)
    r_lo = _dot(xs, w_lo)
    o_ref[...] = r_hi[:8] + r_hi[8:] + r_lo[:8] + b_ref[...]


def _mods(c8, w_ada, b_ada):
    tn = 512
    n = 3 * D_MODEL
    return pl.pallas_call(
        _mods_kernel,
        out_shape=jax.ShapeDtypeStruct((DEPTH, 8, n), F32),
        grid=(DEPTH, n // tn),
        in_specs=[
            pl.BlockSpec((8, D_MODEL), lambda l, j: (0, 0)),
            pl.BlockSpec((None, D_MODEL, tn), lambda l, j: (l, 0, j)),
            pl.BlockSpec((None, 1, tn), lambda l, j: (l, 0, j)),
        ],
        out_specs=pl.BlockSpec((None, 8, tn), lambda l, j: (l, 0, j)),
        compiler_params=_cparams(2, 48),
        name="mods",
    )(c8, w_ada, b_ada.reshape(DEPTH, 1, n))


def _norm_kernel(*refs, modulate):
    if modulate:
        x_ref, g_ref, mod_ref, o_ref = refs
    else:
        x_ref, g_ref, o_ref = refs
    y = _rms(x_ref[...], g_ref[...])
    if modulate:
        shift = mod_ref[:, 0:D_MODEL]
        scale = mod_ref[:, D_MODEL:2 * D_MODEL]
        y = y * (1.0 + scale) + shift
    o_ref[...] = y.astype(o_ref.dtype)


def _norm(x, g, mods4=None, layer=0, row_fn=None, out_dtype=BF16):
    m = x.shape[0]
    modulate = mods4 is not None
    in_specs = [pl.BlockSpec((NORM_TM, D_MODEL), lambda i: (i, 0)),
                pl.BlockSpec((1, D_MODEL), lambda i: (0, 0))]
    args = [x, g]
    if modulate:
        in_specs.append(pl.BlockSpec((None, None, 1, 3 * D_MODEL),
                                     lambda i: (layer, row_fn(i * NORM_TM), 0, 0)))
        args.append(mods4)
    return pl.pallas_call(
        functools.partial(_norm_kernel, modulate=modulate),
        out_shape=jax.ShapeDtypeStruct((m, D_MODEL), out_dtype),
        grid=(m // NORM_TM,),
        in_specs=in_specs,
        out_specs=pl.BlockSpec((NORM_TM, D_MODEL), lambda i: (i, 0)),
        compiler_params=_cparams(1, 32),
        name="norm_mod" if modulate else "norm_final",
    )(*args)


Q_KINDS = ("aq", "cq", "dq")
KV_KINDS = ("plain", "dk", "ckv")


def _proj_kernel(tbl_ref, *refs, kinds, lat):
    del tbl_ref
    if lat:
        (h_ref, w_ref, cqg_ref, ckg_ref, cosc, sloc, shic, cosd, slod, shid,
         zb_ref, g_ref, acc_ref) = refs
        q_out = kv_out = zb_ref
        kv_dt = BF16
    else:
        h_ref, w_ref, cqg_ref, ckg_ref, zq_ref, zf_ref, g_ref, acc_ref = refs
        q_out, kv_out = zq_ref, zf_ref
        kv_dt = F32
    j = pl.program_id(1)
    acc_ref[...] = _dot_nt(h_ref[...], w_ref[...].astype(BF16))

    def rope_c(z):
        return _rope(z, cosc[...], sloc[...], shic[...], 32) if lat else z

    def rope_d(z):
        return _rope(z, cosd[...], slod[...], shid[...], 16) if lat else z

    epilogues = {
        "aq": (q_out, BF16, lambda z, k: z * QS_A),
        "cq": (q_out, BF16, lambda z, k: rope_c(_rms(z, cqg_ref[...])) * QS_A),
        "dq": (q_out, BF16, lambda z, k: rope_d(z) * QS_D),
        "plain": (kv_out, kv_dt, lambda z, k: z),
        "dk": (kv_out, kv_dt, lambda z, k: rope_d(z)),
        "ckv": (kv_out, kv_dt, lambda z, k: rope_c(_rms(z, ckg_ref[...])) if k < 2 else z),
        "gate": (g_ref, F32, lambda z, k: _silu(z)),
    }
    for kind in sorted(set(kinds)):
        out_ref, dt, fn = epilogues[kind]
        cond = functools.reduce(lambda a, c: a | c, [j == t for t, kd in enumerate(kinds) if kd == kind])

        @pl.when(cond)
        def _(out_ref=out_ref, dt=dt, fn=fn):
            for k in range(TN // LANE):
                sl = slice(k * LANE, (k + 1) * LANE)
                out_ref[:, sl] = fn(acc_ref[:, sl], k).astype(dt)


def _resident_table(kinds, groups):
    rows = []
    for group in groups:
        writes = [t for t, kd in enumerate(kinds) if kd in group]
        pos = {t: n for n, t in enumerate(writes)}
        row, cur = [], 0
        for t in range(len(kinds)):
            if t in pos:
                cur = pos[t]
            row.append(cur)
        rows.append(row)
    return rows


ROW_UNIT = 64


def _wt_spec(rows, index_map):
    return pl.BlockSpec((None, pl.Element(rows), pl.Element(D_MODEL)), index_map)


def _proj(h, w_t, layer, tiles, cq_gain, ck_gain, tables, lat, name):
    m = h.shape[0]
    kinds = tuple(kd for kd, _ in tiles)
    groups = [Q_KINDS + KV_KINDS, ("gate",)] if lat else [Q_KINDS, KV_KINDS, ("gate",)]
    n_out = len(groups)
    assert all(r % ROW_UNIT == 0 for _, r in tiles)
    tbl = np.asarray(_resident_table(kinds, groups) + [[r // ROW_UNIT for _, r in tiles]], np.int32)
    counts = [sum(kd in grp for kd in kinds) for grp in groups]
    in_specs = [
        pl.BlockSpec((TM, D_MODEL), lambda i, j, t: (i, 0)),
        _wt_spec(TN, lambda i, j, t: (layer, t[n_out, j] * ROW_UNIT, 0)),
        pl.BlockSpec((1, LANE), lambda i, j, t: (0, 0)),
        pl.BlockSpec((1, LANE), lambda i, j, t: (0, 0)),
    ]
    args = [h, w_t, cq_gain, ck_gain]
    if lat:
        t_blocks = tables[0].shape[0] // TM
        for tab in tables:
            in_specs.append(pl.BlockSpec((TM, LANE), lambda i, j, t: (i % t_blocks, 0)))
            args.append(tab)
    dtypes = [BF16, F32] if lat else [BF16, F32, F32]
    out_shape = tuple(jax.ShapeDtypeStruct((m, n * TN), dt) for n, dt in zip(counts, dtypes))
    out_specs = tuple(pl.BlockSpec((TM, TN), lambda i, j, t, r=r: (i, t[r, j])) for r in range(n_out))
    return pl.pallas_call(
        functools.partial(_proj_kernel, kinds=kinds, lat=lat),
        out_shape=out_shape,
        grid_spec=pltpu.PrefetchScalarGridSpec(
            num_scalar_prefetch=1,
            grid=(m // TM, len(tiles)),
            in_specs=in_specs,
            out_specs=out_specs,
            scratch_shapes=[pltpu.VMEM((TM, TN), F32)]),
        compiler_params=_cparams(2, 60),
        name=name,
    )(jnp.asarray(tbl), *args)


COL = {}
_off = 0
for _name, _size in (("a_q", 1024), ("a_k", 1024), ("a_v", 1024), ("a_z", 1024),
                     ("b_cq", B_Q_LORA), ("b_ckv", B_KV_LORA), ("b_kpe", B_ROPE), ("b_z", 1024),
                     ("c_q", 1024), ("c_k", 256), ("c_v", 256), ("c_z", 1024),
                     ("d_q", 1024), ("d_k", 1024), ("d_v", 1024), ("d_z", 1024)):
    COL[_name] = _off
    _off += _size


def _seg_tiles(kind, name, n_tiles):
    return [(kind, COL[name] + n * TN) for n in range(n_tiles)]


MAIN_TILES = (_seg_tiles("aq", "a_q", 2) + _seg_tiles("cq", "c_q", 2) + _seg_tiles("dq", "d_q", 2)
              + _seg_tiles("plain", "a_k", 2) + _seg_tiles("plain", "a_v", 2)
              + _seg_tiles("dk", "d_k", 2) + _seg_tiles("plain", "d_v", 2) + _seg_tiles("ckv", "c_k", 1)
              + _seg_tiles("gate", "a_z", 2) + _seg_tiles("gate", "b_z", 2)
              + _seg_tiles("gate", "c_z", 2) + _seg_tiles("gate", "d_z", 2))


def _bq_kernel(*refs, lat):
    if lat:
        h_ref, w_ref, g_ref, wuq_ref, cosd, slod, shid, o_ref = refs
    else:
        h_ref, w_ref, g_ref, wuq_ref, o_ref = refs
    z = _dot_nt(h_ref[...], w_ref[...].astype(BF16))
    cq = _rms(z, g_ref[...]).astype(BF16)
    q = _dot(cq, wuq_ref[...])
    for hh in range(N_HEADS):
        c0 = hh * 2 * LANE
        o_ref[:, c0:c0 + LANE] = (q[:, c0:c0 + LANE] * QS_B).astype(BF16)
        pe = q[:, c0 + LANE:c0 + 2 * LANE]
        if lat:
            pe = _rope(pe, cosd[...], slod[...], shid[...], 16)
        o_ref[:, c0 + LANE:c0 + 2 * LANE] = (pe * QS_B).astype(BF16)


def _bq_proj(h, w_t, layer, gain, w_uq, tables_d, lat):
    m = h.shape[0]
    tm = TM_B
    in_specs = [
        pl.BlockSpec((tm, D_MODEL), lambda i: (i, 0)),
        _wt_spec(B_Q_LORA, lambda i: (layer, COL["b_cq"], 0)),
        pl.BlockSpec((1, B_Q_LORA), lambda i: (0, 0)),
        pl.BlockSpec((B_Q_LORA, N_HEADS * 2 * LANE), lambda i: (0, 0)),
    ]
    args = [h, w_t, gain, w_uq]
    if lat:
        t_blocks = tables_d[0].shape[0] // tm
        for t in tables_d:
            in_specs.append(pl.BlockSpec((tm, LANE), lambda i: (i % t_blocks, 0)))
            args.append(t)
    return pl.pallas_call(
        functools.partial(_bq_kernel, lat=lat),
        out_shape=jax.ShapeDtypeStruct((m, N_HEADS * 2 * LANE), BF16),
        grid=(m // tm,),
        in_specs=in_specs,
        out_specs=pl.BlockSpec((tm, N_HEADS * 2 * LANE), lambda i: (i, 0)),
        compiler_params=_cparams(1, 56),
        name="bq_proj_lat" if lat else "bq_proj_ctx",
    )(*args)


def _expand_kv(ckv, kpe, wukv_ref, kb_ref, vb_ref):
    kv = _dot(ckv.astype(BF16), wukv_ref[...])
    kpe_b = kpe.astype(BF16)
    for hh in range(N_HEADS):
        c0 = hh * 2 * LANE
        kb_ref[:, c0:c0 + LANE] = kv[:, c0:c0 + LANE].astype(BF16)
        kb_ref[:, c0 + LANE:c0 + 2 * LANE] = kpe_b
        vb_ref[:, hh * LANE:(hh + 1) * LANE] = kv[:, c0 + LANE:c0 + 2 * LANE].astype(BF16)


def _bkv_kernel(*refs, lat):
    if lat:
        h_ref, w_ref, wpe_ref, g_ref, wukv_ref, cosd, slod, shid, kb_ref, vb_ref = refs
    else:
        h_ref, w_ref, wpe_ref, g_ref, wukv_ref, kb_ref, vb_ref, ckv_ref, kpe_ref = refs
    h = h_ref[...]
    ckv = _rms(_dot_nt(h, w_ref[...].astype(BF16)), g_ref[...])
    kpe = _dot_nt(h, wpe_ref[...].astype(BF16))
    lane = lax.broadcasted_iota(jnp.int32, (1, LANE), 1)
    kpe = jnp.where(lane < B_ROPE, kpe, 0.0)
    if lat:
        kpe = _rope(kpe, cosd[...], slod[...], shid[...], 16)
    else:
        ckv_ref[...] = ckv
        kpe_ref[...] = kpe[:, :B_ROPE]
    _expand_kv(ckv, kpe, wukv_ref, kb_ref, vb_ref)


def _bkv_proj(h, w_t, layer, gain, w_ukv, tables_d, lat):
    m = h.shape[0]
    tm = TM_B
    in_specs = [
        pl.BlockSpec((tm, D_MODEL), lambda i: (i, 0)),
        _wt_spec(B_KV_LORA, lambda i: (layer, COL["b_ckv"], 0)),
        _wt_spec(LANE, lambda i: (layer, COL["b_kpe"], 0)),
        pl.BlockSpec((1, B_KV_LORA), lambda i: (0, 0)),
        pl.BlockSpec((B_KV_LORA, N_HEADS * 2 * LANE), lambda i: (0, 0)),
    ]
    args = [h, w_t, w_t, gain, w_ukv]
    out_shape = [jax.ShapeDtypeStruct((m, N_HEADS * 2 * LANE), BF16),
                 jax.ShapeDtypeStruct((m, GROUP_W), BF16)]
    out_specs = [pl.BlockSpec((tm, N_HEADS * 2 * LANE), lambda i: (i, 0)),
                 pl.BlockSpec((tm, GROUP_W), lambda i: (i, 0))]
    if lat:
        t_blocks = tables_d[0].shape[0] // tm
        for t in tables_d:
            in_specs.append(pl.BlockSpec((tm, LANE), lambda i: (i % t_blocks, 0)))
            args.append(t)
    else:
        out_shape += [jax.ShapeDtypeStruct((m, B_KV_LORA), F32),
                      jax.ShapeDtypeStruct((m, B_ROPE), F32)]
        out_specs += [pl.BlockSpec((tm, B_KV_LORA), lambda i: (i, 0)),
                      pl.BlockSpec((tm, B_ROPE), lambda i: (i, 0))]
    return pl.pallas_call(
        functools.partial(_bkv_kernel, lat=lat),
        out_shape=tuple(out_shape),
        grid=(m // tm,),
        in_specs=in_specs,
        out_specs=tuple(out_specs),
        compiler_params=_cparams(1, 56),
        name="bkv_proj_lat" if lat else "bkv_proj_ctx",
    )(*args)


def _cache_kv_kernel(ckv_ref, kpe_ref, wukv_ref, kb_ref, vb_ref):
    _expand_kv(ckv_ref[...], kpe_ref[...], wukv_ref, kb_ref, vb_ref)


def _cache_kv(cache_ckv, cache_kpe128, w_ukv, layer):
    nb, _, s, _ = cache_ckv.shape
    return pl.pallas_call(
        _cache_kv_kernel,
        out_shape=(jax.ShapeDtypeStruct((nb * s, N_HEADS * 2 * LANE), BF16),
                   jax.ShapeDtypeStruct((nb * s, GROUP_W), BF16)),
        grid=(nb,),
        in_specs=[
            pl.BlockSpec((None, None, s, B_KV_LORA), lambda b: (b, layer, 0, 0)),
            pl.BlockSpec((None, None, s, LANE), lambda b: (b, layer, 0, 0)),
            pl.BlockSpec((B_KV_LORA, N_HEADS * 2 * LANE), lambda b: (0, 0)),
        ],
        out_specs=(pl.BlockSpec((s, N_HEADS * 2 * LANE), lambda b: (b, 0)),
                   pl.BlockSpec((s, GROUP_W), lambda b: (b, 0))),
        compiler_params=_cparams(1, 32),
        name="cache_kv",
    )(cache_ckv, cache_kpe128, w_ukv)


def _attn_kernel(*refs, n_src, nh, group, dk, n_chunks, diff, lam_init, k_staged):
    q_ref = refs[0]
    kv_refs = refs[1:1 + 2 * n_src]
    g_ref = refs[1 + 2 * n_src]
    pos = 2 + 2 * n_src
    if diff:
        lq1, lk1, lq2, lk2, subln = refs[pos:pos + 5]
        pos += 5
        lam = (jnp.exp(jnp.sum(lq1[...] * lk1[...], axis=-1, keepdims=True))
               - jnp.exp(jnp.sum(lq2[...] * lk2[...], axis=-1, keepdims=True)) + lam_init)
        lane = lax.broadcasted_iota(jnp.int32, (1, LANE), 1)
        m_lo = jnp.where(lane < D_QK, 1.0, 0.0).astype(BF16)
        m_hi = jnp.where(lane >= D_QK, 1.0, 0.0).astype(BF16)
    o_ref = refs[pos]
    scratch = list(refs[pos + 1:])
    n_kvh = nh // group

    k_refs, vt_refs = [], []
    for s in range(n_src):
        k_ref, v_ref = kv_refs[2 * s], kv_refs[2 * s + 1]
        if k_staged[s]:
            kb_ref = scratch.pop(0)
            kb_ref[...] = k_ref[...].astype(BF16)
            k_ref = kb_ref
        vt_ref = scratch.pop(0)
        for kh in range(n_kvh):
            v = v_ref[:, kh * LANE:(kh + 1) * LANE].astype(F32)
            vt_ref[kh * LANE:(kh + 1) * LANE, :] = v.T.astype(BF16)
        k_refs.append(k_ref)
        vt_refs.append(vt_ref)

    s_scr, p_scr, m_scr, l_scr = scratch[:4]
    n_w = 2 if diff else 1

    def rows_of(c):
        if isinstance(c, int):
            return slice(c * Q_CHUNK, (c + 1) * Q_CHUNK)
        return pl.ds(pl.multiple_of(c * Q_CHUNK, Q_CHUNK), Q_CHUNK)

    def qk(c, j):
        hh, w = divmod(j, n_w)
        kh = hh // group
        q = q_ref[rows_of(c), hh * dk:(hh + 1) * dk]
        if diff:
            q = q * (m_lo if w == 0 else m_hi)
        ks = [r[:, kh * dk:(kh + 1) * dk] for r in k_refs]
        _stage_qk(q, ks, [None] * n_src, s_scr.at[j % 2], m_scr.at[j % 2])

    def sm(c, j):
        _stage_softmax(s_scr.at[j % 2], m_scr.at[j % 2], p_scr.at[j % 2], l_scr.at[j % 2])

    def pv(c, j):
        hh, w = divmod(j, n_w)
        kh = hh // group
        vts = [r[kh * LANE:(kh + 1) * LANE, :] for r in vt_refs]
        o_t = _stage_pv(vts, p_scr.at[j % 2], l_scr.at[j % 2])
        if diff:
            o1_scr = scratch[4]
            if w == 0:
                o1_scr[...] = o_t
                return
            o = _rms((o1_scr[...] - lam * o_t).T, subln[...]) * (1.0 - lam_init)
        else:
            o = o_t.T
        rows = rows_of(c)
        osl = slice(hh * LANE, (hh + 1) * LANE)
        o_ref[rows, osl] = (o * g_ref[rows, osl]).astype(BF16)

    _pipeline(n_chunks, nh * n_w, qk, sm, pv)


def _attn(q, q_spec, srcs, g, g_spec, out_rows, out_spec, grid, *, nh, group, dk, t_q,
          diff_params=None, lam_init=0.0, name):
    in_specs = [q_spec]
    args = [q]
    scratch_shapes = []
    k_staged = []
    for ka, kspec, va, vspec in srcs:
        in_specs += [kspec, vspec]
        args += [ka, va]
        s_len, k_w = kspec.block_shape[-2:]
        v_w = vspec.block_shape[-1]
        k_staged.append(ka.dtype != BF16)
        if k_staged[-1]:
            scratch_shapes.append(pltpu.VMEM((s_len, k_w), BF16))
        scratch_shapes.append(pltpu.VMEM((v_w, s_len), BF16))
    in_specs.append(g_spec)
    args.append(g)
    s_total = sum(spec.block_shape[-2] for _, spec, _, _ in srcs)
    scratch_shapes += [pltpu.VMEM((2, s_total, Q_CHUNK), F32), pltpu.VMEM((2, s_total, Q_CHUNK), BF16),
                       pltpu.VMEM((2, 1, Q_CHUNK), F32), pltpu.VMEM((2, 1, Q_CHUNK), F32)]
    diff = diff_params is not None
    if diff:
        scratch_shapes.append(pltpu.VMEM((LANE, Q_CHUNK), F32))
        for p in diff_params:
            in_specs.append(pl.BlockSpec(p.shape, lambda *idx: (0, 0)))
            args.append(p)
    return pl.pallas_call(
        functools.partial(_attn_kernel, n_src=len(srcs), nh=nh, group=group, dk=dk,
                          n_chunks=t_q // Q_CHUNK, diff=diff, lam_init=lam_init,
                          k_staged=tuple(k_staged)),
        out_shape=jax.ShapeDtypeStruct((out_rows, GROUP_W), BF16),
        grid=grid,
        in_specs=in_specs,
        out_specs=out_spec,
        scratch_shapes=scratch_shapes,
        compiler_params=_cparams(len(grid), 48),
        name=name,
    )(*args)


NAT_ROWS_PER_CHUNK = Q_CHUNK // GRID_W
NAT_WIN_CHUNKS = 3
NAT_N_DR = 2 * A_KH - 1
NAT_N_DC = 2 * A_KW - 1


def _natten_kernel(rpb_ref, q_ref, k_ref, v_ref, kc_ref, vc_ref, g_ref, o_ref,
                   tl_ref, tr_ref, vt_ref, kcb_ref, vct_ref, s_scr, p_scr, m_scr, l_scr, *, n_rows):
    h = pl.program_id(0)
    b = pl.program_id(1)
    n_chunks = q_ref.shape[0] // Q_CHUNK

    @pl.when(b == 0)
    def _build():
        ck = lax.broadcasted_iota(jnp.int32, (GRID_W, LANE), 0)
        lane = lax.broadcasted_iota(jnp.int32, (GRID_W, LANE), 1)
        left = lane < GRID_W
        cq = jnp.where(left, lane, lane - GRID_W)
        delta = jnp.clip(ck - cq, -(A_KW - 1), A_KW - 1) + (A_KW - 1)
        start_c = jnp.clip(cq - A_KW // 2, 0, GRID_W - A_KW)
        col_ok = (ck >= start_c) & (ck < start_c + A_KW)
        for d in range(NAT_N_DR):
            t = jnp.zeros((GRID_W, LANE), F32)
            for e in range(NAT_N_DC):
                t = jnp.where(delta == e, rpb_ref[h * (NAT_N_DR * NAT_N_DC) + d * NAT_N_DC + e], t)
            t = jnp.where(col_ok, t * LOG2E, MASK_NEG)
            tl_ref[d] = jnp.where(left, t, 0.0)
            tr_ref[d] = jnp.where(left, 0.0, t)
        tl_ref[NAT_N_DR] = jnp.where(left, MASK_NEG, 0.0)
        tr_ref[NAT_N_DR] = jnp.where(left, 0.0, MASK_NEG)

    for j in range(n_chunks):
        vt_ref[j] = v_ref[j * Q_CHUNK:(j + 1) * Q_CHUNK, :].astype(F32).T.astype(BF16)
    kcb_ref[...] = kc_ref[...].astype(BF16)
    vct_ref[...] = vc_ref[...].T.astype(BF16)
    kh = min(A_KH, n_rows)

    def chunk_of(ci, j):
        return 2 * ci + j

    def rows_of(c):
        if isinstance(c, int):
            return slice(c * Q_CHUNK, (c + 1) * Q_CHUNK)
        return pl.ds(pl.multiple_of(c * Q_CHUNK, Q_CHUNK), Q_CHUNK)

    def window(c):
        return jnp.clip(c - 1, 0, n_chunks - NAT_WIN_CHUNKS)

    def qk(ci, j):
        c = chunk_of(ci, j)
        r0 = c * NAT_ROWS_PER_CHUNK
        w0 = window(c)
        ws = w0 * NAT_ROWS_PER_CHUNK
        ks, biases = [], []
        for w in range(NAT_WIN_CHUNKS):
            ks.append(k_ref[rows_of(w0 + w), :])
            blk_rows = []
            for jk in range(NAT_ROWS_PER_CHUNK):
                rk = ws + w * NAT_ROWS_PER_CHUNK + jk
                blks = []
                for qp in range(NAT_ROWS_PER_CHUNK // 2):
                    idx = []
                    for half in range(2):
                        rq = r0 + 2 * qp + half
                        st = jnp.clip(rq - kh // 2, 0, n_rows - kh)
                        valid = (rk >= st) & (rk < st + kh)
                        idx.append(jnp.where(valid, rk - rq + (A_KH - 1), NAT_N_DR))
                    blks.append(tl_ref[idx[0]] + tr_ref[idx[1]])
                blk_rows.append(jnp.concatenate(blks, axis=1))
            biases.append(jnp.concatenate(blk_rows, axis=0))
        ks.append(kcb_ref[...])
        biases.append(None)
        _stage_qk(q_ref[rows_of(c), :], ks, biases, s_scr.at[j], m_scr.at[j])

    def sm(ci, j):
        _stage_softmax(s_scr.at[j], m_scr.at[j], p_scr.at[j], l_scr.at[j])

    def pv(ci, j):
        c = chunk_of(ci, j)
        w0 = window(c)
        vts = [vt_ref[w0 + w] for w in range(NAT_WIN_CHUNKS)] + [vct_ref[...]]
        o = _stage_pv(vts, p_scr.at[j], l_scr.at[j]).T
        rows = rows_of(c)
        o_ref[rows, :] = (o * g_ref[rows, :]).astype(BF16)

    _pipeline(n_chunks // 2, 2, qk, sm, pv)


def _natten(rpb_flat, zb, cache_k, cache_v, g, layer, nb, t):
    s_ctx = cache_k.shape[2]
    n_rows = t // GRID_W
    return pl.pallas_call(
        functools.partial(_natten_kernel, n_rows=n_rows),
        out_shape=jax.ShapeDtypeStruct((nb * t, GROUP_W), BF16),
        grid=(N_HEADS, nb),
        in_specs=[
            pl.BlockSpec(memory_space=pltpu.SMEM),
            pl.BlockSpec((t, LANE), lambda h, b: (b, h)),
            pl.BlockSpec((t, LANE), lambda h, b: (b, 24 + h)),
            pl.BlockSpec((t, LANE), lambda h, b: (b, 32 + h)),
            pl.BlockSpec((None, None, s_ctx, LANE), lambda h, b: (b, layer, 0, h)),
            pl.BlockSpec((None, None, s_ctx, LANE), lambda h, b: (b, layer, 0, h)),
            pl.BlockSpec((t, LANE), lambda h, b: (b, h)),
        ],
        out_specs=pl.BlockSpec((t, LANE), lambda h, b: (b, h)),
        scratch_shapes=[pltpu.VMEM((NAT_N_DR + 1, GRID_W, LANE), F32),
                        pltpu.VMEM((NAT_N_DR + 1, GRID_W, LANE), F32),
                        pltpu.VMEM((t // Q_CHUNK, LANE, Q_CHUNK), BF16),
                        pltpu.VMEM((s_ctx, LANE), BF16),
                        pltpu.VMEM((LANE, s_ctx), BF16),
                        pltpu.VMEM((2, NAT_WIN_CHUNKS * Q_CHUNK + s_ctx, Q_CHUNK), F32),
                        pltpu.VMEM((2, NAT_WIN_CHUNKS * Q_CHUNK + s_ctx, Q_CHUNK), BF16),
                        pltpu.VMEM((2, 1, Q_CHUNK), F32),
                        pltpu.VMEM((2, 1, Q_CHUNK), F32)],
        compiler_params=_cparams(2, 48),
        name="natten",
    )(rpb_flat, zb, zb, zb, cache_k, cache_v, g)


def _outproj_kernel(ya, yb, yc, yd, w_ref, x_ref, gate_ref, o_ref):
    acc = None
    for n, y in enumerate((ya, yb, yc, yd)):
        part = _dot(y[...], w_ref[n * GROUP_W:(n + 1) * GROUP_W, :].astype(BF16))
        acc = part if acc is None else acc + part
    o_ref[...] = x_ref[...] + gate_ref[...] * acc


def _outproj(ys, w_out, x, mods4, layer, row_fn):
    m = x.shape[0]
    gate_blk0 = 2 * D_MODEL // TN
    y_spec = pl.BlockSpec((TM, GROUP_W), lambda i, j: (i, 0))
    return pl.pallas_call(
        _outproj_kernel,
        out_shape=jax.ShapeDtypeStruct((m, D_MODEL), F32),
        grid=(m // TM, D_MODEL // TN),
        in_specs=[y_spec, y_spec, y_spec, y_spec,
                  pl.BlockSpec((None, D_MODEL, TN), lambda i, j: (layer, 0, j)),
                  pl.BlockSpec((TM, TN), lambda i, j: (i, j)),
                  pl.BlockSpec((None, None, 1, TN),
                               lambda i, j: (layer, row_fn(i * TM), 0, gate_blk0 + j))],
        out_specs=pl.BlockSpec((TM, TN), lambda i, j: (i, j)),
        compiler_params=_cparams(2, 56),
        name="outproj",
    )(*ys, w_out, x, mods4)


def _axial_tables(t, head_dim):
    pos = np.arange(t)
    part = head_dim // 2
    half = part // 2
    freqs = ROPE_THETA ** (-(np.arange(half, dtype=np.float64) * 2.0 / part))
    lane = np.arange(LANE) % head_dim
    p = np.where(lane < part, (pos // GRID_W)[:, None], (pos % GRID_W)[:, None])
    within = lane % part
    ang = p * freqs[within % half][None, :]
    first = (within < half)[None, :]
    cos = np.cos(ang)
    sin = np.sin(ang)
    return (jnp.asarray(cos, F32), jnp.asarray(np.where(first, -sin, 0.0), F32),
            jnp.asarray(np.where(first, 0.0, sin), F32))


def _ctx_layer(x, l, lam_init, w_t, w_out, w_uq, w_ukv, p, mods4, nb, t):
    row_fn = lambda r: 0
    h = _norm(x, p["norm_g"], mods4, l, row_fn)
    zq, zf, g = _proj(h, w_t, l, MAIN_TILES, p["c_q_norm"], p["c_k_norm"], None, False, "main_proj_ctx")
    qb = _bq_proj(h, w_t, l, p["b_q_norm"], w_uq, None, lat=False)
    kb, vb, ckv_state, kpe_state = _bkv_proj(h, w_t, l, p["b_kv_norm"], w_ukv, None, lat=False)

    grid = (nb,)
    wide = lambda blk: pl.BlockSpec((t, GROUP_W), lambda b: (b, blk))
    out_spec = pl.BlockSpec((t, GROUP_W), lambda b: (b, 0))
    m = nb * t
    ya = _attn(zq, wide(0), [(zf, wide(0), zf, wide(1))], g, wide(0), m, out_spec, grid,
               nh=N_HEADS, group=1, dk=LANE, t_q=t, name="attn_a_ctx")
    yb = _attn(qb, pl.BlockSpec((t, 2 * GROUP_W), lambda b: (b, 0)),
               [(kb, pl.BlockSpec((t, 2 * GROUP_W), lambda b: (b, 0)), vb, wide(0))],
               g, wide(1), m, out_spec, grid, nh=N_HEADS, group=1, dk=2 * LANE, t_q=t,
               name="attn_b_ctx")
    ckv_w = C_KV_HEADS * HEAD_DIM
    c_k_blk = 4 * GROUP_W // ckv_w
    yc = _attn(zq, wide(1),
               [(zf, pl.BlockSpec((t, ckv_w), lambda b: (b, c_k_blk)),
                 zf, pl.BlockSpec((t, ckv_w), lambda b: (b, c_k_blk + 1)))],
               g, wide(2), m, out_spec, grid, nh=N_HEADS, group=N_HEADS // C_KV_HEADS,
               dk=LANE, t_q=t, name="attn_c_ctx")
    yd = _attn(zq, wide(2), [(zf, wide(2), zf, wide(3))], g, wide(3), m, out_spec, grid,
               nh=N_HEADS, group=1, dk=LANE, t_q=t,
               diff_params=p["diff"], lam_init=lam_init, name="attn_d_ctx")
    x_new = _outproj((ya, yb, yc, yd), w_out, x, mods4, l, row_fn)
    states = dict(
        a_k=zf[:, 0:GROUP_W], a_v=zf[:, GROUP_W:2 * GROUP_W],
        d_k=zf[:, 2 * GROUP_W:3 * GROUP_W], d_v=zf[:, 3 * GROUP_W:4 * GROUP_W],
        c_k=zf[:, 4 * GROUP_W:4 * GROUP_W + ckv_w], c_v=zf[:, 4 * GROUP_W + ckv_w:4 * GROUP_W + 2 * ckv_w],
        b_ckv=ckv_state, b_kpe=kpe_state)
    return x_new, states


def _lat_layer(x, l, lam_init, w_t, w_out, w_uq, w_ukv, p, mods4, caches, tables_c, tables_d, nb, t):
    row_fn = lambda r: 1 + r // t
    h = _norm(x, p["norm_g"], mods4, l, row_fn)
    zb, g = _proj(h, w_t, l, MAIN_TILES, p["c_q_norm"], p["c_k_norm"], tables_c + tables_d, True,
                  "main_proj_lat")
    qb = _bq_proj(h, w_t, l, p["b_q_norm"], w_uq, tables_d, lat=True)
    kb, vb = _bkv_proj(h, w_t, l, p["b_kv_norm"], w_ukv, tables_d, lat=True)
    kbc, vbc = _cache_kv(caches["b_ckv"], caches["b_kpe"], w_ukv, l)
    s_ctx = caches["a_k"].shape[2]
    m = nb * t

    ya = _natten(p["rpb"], zb, caches["a_k"], caches["a_v"], g, l, nb, t)

    hps = LAT_HEADS_PER_STEP
    grid = (nb, N_HEADS // hps)
    col = lambda blk0, w=hps: pl.BlockSpec((t, w * LANE), lambda b, h: (b, blk0 // w + h))
    cache = lambda w=hps: pl.BlockSpec((None, None, s_ctx, w * LANE), lambda b, h: (b, l, 0, h))
    out_spec = col(0)
    yb = _attn(qb, col(0, 2 * hps),
               [(kb, col(0, 2 * hps), vb, col(0)),
                (kbc, pl.BlockSpec((s_ctx, 2 * hps * LANE), lambda b, h: (b, h)),
                 vbc, pl.BlockSpec((s_ctx, hps * LANE), lambda b, h: (b, h)))],
               g, col(8), m, out_spec, grid, nh=hps, group=1, dk=2 * LANE, t_q=t, name="attn_b_lat")
    yd = _attn(zb, col(16),
               [(zb, col(40), zb, col(48)),
                (caches["d_k"], cache(), caches["d_v"], cache())],
               g, col(24), m, out_spec, grid, nh=hps, group=1, dk=LANE, t_q=t,
               diff_params=p["diff"], lam_init=lam_init, name="attn_d_lat")
    grp = N_HEADS // C_KV_HEADS
    yc = _attn(zb, col(8, grp),
               [(zb, col(56, 1), zb, col(58, 1)),
                (caches["c_k"], cache(1), caches["c_v"], cache(1))],
               g, col(16, grp), m, col(0, grp), (nb, C_KV_HEADS), nh=grp, group=grp, dk=LANE, t_q=t,
               name="attn_c_lat")
    return _outproj((ya, yb, yc, yd), w_out, x, mods4, l, row_fn)


def kernel(x_prompt, x_sample, cache_a_k, cache_a_v, cache_b_ckv, cache_b_kpe, cache_c_k, cache_c_v, cache_d_k, cache_d_v, c, c_ctx, norm_g, w_ada, b_ada, w_in, w_out, a_rpb, b_q_norm, b_w_uq, b_kv_norm, b_w_ukv, c_q_norm, c_k_norm, d_lq1, d_lk1, d_lq2, d_lk2, d_subln, final_norm_g):
    nb_c, t_c, _ = x_prompt.shape
    nb_l, t_l, _ = x_sample.shape
    s_ctx = cache_a_k.shape[2]

    c8 = jnp.concatenate([c_ctx[None, :], c, jnp.zeros((8 - 1 - nb_l, D_MODEL), F32)], axis=0)
    mods = _mods(c8, w_ada, b_ada)
    mods4 = mods[:, :1 + nb_l].reshape(DEPTH, 1 + nb_l, 1, 3 * D_MODEL)

    w_t = jnp.swapaxes(w_in, 1, 2)
    w_uq = jnp.pad(b_w_uq.reshape(DEPTH, B_Q_LORA, N_HEADS, B_NOPE + B_ROPE),
                   ((0, 0), (0, 0), (0, 0), (0, 2 * LANE - B_NOPE - B_ROPE)))
    w_uq = w_uq.reshape(DEPTH, B_Q_LORA, N_HEADS * 2 * LANE).astype(BF16)
    w_ukv = b_w_ukv.astype(BF16)

    caches = dict(
        a_k=cache_a_k.reshape(nb_l, DEPTH, s_ctx, GROUP_W),
        a_v=cache_a_v.reshape(nb_l, DEPTH, s_ctx, GROUP_W),
        b_ckv=cache_b_ckv,
        b_kpe=jnp.pad(cache_b_kpe, ((0, 0), (0, 0), (0, 0), (0, LANE - B_ROPE))),
        c_k=cache_c_k.reshape(nb_l, DEPTH, s_ctx, C_KV_HEADS * HEAD_DIM),
        c_v=cache_c_v.reshape(nb_l, DEPTH, s_ctx, C_KV_HEADS * HEAD_DIM),
        d_k=cache_d_k.reshape(nb_l, DEPTH, s_ctx, GROUP_W),
        d_v=cache_d_v.reshape(nb_l, DEPTH, s_ctx, GROUP_W),
    )
    tables_c = _axial_tables(t_l, HEAD_DIM)
    tables_d = _axial_tables(t_l, D_QK)

    xp = x_prompt.reshape(nb_c * t_c, D_MODEL)
    xs = x_sample.reshape(nb_l * t_l, D_MODEL)
    ctx_states = []
    for l in range(DEPTH):
        lam_init = 0.8 - 0.6 * math.exp(-0.3 * l)
        p = dict(
            norm_g=norm_g[l][None, :], c_q_norm=c_q_norm[l][None, :], c_k_norm=c_k_norm[l][None, :],
            b_q_norm=b_q_norm[l][None, :], b_kv_norm=b_kv_norm[l][None, :],
            rpb=a_rpb[l].reshape(-1),
            diff=(d_lq1[l][None, :], d_lk1[l][None, :], d_lq2[l][None, :], d_lk2[l][None, :],
                  d_subln[l][None, :]),
        )
        xp, st = _ctx_layer(xp, l, lam_init, w_t, w_out, w_uq[l], w_ukv[l], p, mods4, nb_c, t_c)
        ctx_states.append(st)
        xs = _lat_layer(xs, l, lam_init, w_t, w_out, w_uq[l], w_ukv[l], p, mods4, caches,
                        tables_c, tables_d, nb_l, t_l)

    fg = final_norm_g[None, :]
    y_prompt = _norm(xp, fg, out_dtype=F32).reshape(nb_c, t_c, D_MODEL)
    y_sample = _norm(xs, fg, out_dtype=F32).reshape(nb_l, t_l, D_MODEL)

    def stack(name, tail):
        return jnp.stack([s[name].reshape((nb_c, t_c) + tail) for s in ctx_states], axis=1)

    return (y_prompt, y_sample,
            stack("a_k", (N_HEADS, HEAD_DIM)), stack("a_v", (N_HEADS, HEAD_DIM)),
            stack("b_ckv", (B_KV_LORA,)), stack("b_kpe", (B_ROPE,)),
            stack("c_k", (C_KV_HEADS, HEAD_DIM)), stack("c_v", (C_KV_HEADS, HEAD_DIM)),
            stack("d_k", (N_HEADS, 2, D_QK)), stack("d_v", (N_HEADS, HEAD_DIM)))
```

```python
import functools
import math

import numpy as np
import jax
import jax.numpy as jnp
from jax import lax
from jax.experimental import pallas as pl
from jax.experimental.pallas import tpu as pltpu

F32 = jnp.float32
BF16 = jnp.bfloat16

D_MODEL = 4096
DEPTH = 2
GRID_W = 64
HEAD_DIM = 128
N_HEADS = 8
GROUP_W = N_HEADS * HEAD_DIM
A_KH = 8
A_KW = 16
B_NOPE = 128
B_ROPE = 64
B_Q_LORA = 768
B_KV_LORA = 512
C_KV_HEADS = 2
D_QK = 64
ROPE_THETA = 10000.0
EPS = 1e-6

LANE = 128
MASK_NEG = -1e30

LOG2E = math.log2(math.e)
QS_A = HEAD_DIM ** -0.5 * LOG2E
QS_B = (B_NOPE + B_ROPE) ** -0.5 * LOG2E
QS_D = D_QK ** -0.5 * LOG2E

TM = 1024
TN = 512
TM_B = 512
Q_CHUNK = 256
LAT_HEADS_PER_STEP = 2
NORM_TM = 256
NORM_ROWS = 16


def _cparams(n_axes, vmem_mib):
    return pltpu.CompilerParams(dimension_semantics=("arbitrary",) * n_axes,
                                vmem_limit_bytes=vmem_mib * 2 ** 20)


def _rms(z, g):
    ms = jnp.mean(z * z, axis=-1, keepdims=True)
    return z * lax.rsqrt(ms + EPS) * g


def _silu(z):
    return z * jax.nn.sigmoid(z)


def _rope(z, cos, sin_lo, sin_hi, half):
    return (z * cos + pltpu.roll(z, LANE - half, 1) * sin_lo
            + pltpu.roll(z, half, 1) * sin_hi)


def _dot(a, b):
    return jnp.dot(a, b, preferred_element_type=F32)


def _dot_nt(a, b):
    return lax.dot_general(a, b, (((1,), (1,)), ((), ())), preferred_element_type=F32)


def _stage_qk(q, ks, biases, s_ref, m_ref):
    off = 0
    m = None
    for k, bias in zip(ks, biases):
        st = _dot_nt(k, q)
        if bias is not None:
            st = st + bias
        s_ref[off:off + k.shape[0], :] = st
        mi = st.max(axis=0, keepdims=True)
        m = mi if m is None else jnp.maximum(m, mi)
        off += k.shape[0]
    m_ref[...] = m


def _stage_softmax(s_ref, m_ref, p_ref, l_ref):
    p = jnp.exp2(s_ref[...] - m_ref[...])
    l_ref[...] = p.sum(axis=0, keepdims=True)
    p_ref[...] = p.astype(BF16)


def _stage_pv(vts, p_ref, l_ref):
    off = 0
    o_t = None
    for vt in vts:
        part = _dot(vt, p_ref[off:off + vt.shape[1], :])
        o_t = part if o_t is None else o_t + part
        off += vt.shape[1]
    return o_t / l_ref[...]


def _pipeline(n_chunks, cpc, qk, sm, pv):
    assert cpc % 2 == 0

    def prev(c, j, back):
        return (c, j - back) if j >= back else (c - 1, j - back + cpc)

    for j in range(cpc):
        qk(0, j)
        if j >= 1:
            sm(0, j - 1)
        if j >= 2:
            pv(0, j - 2)
    if n_chunks > 1:
        def body(c, carry):
            for j in range(cpc):
                qk(c, j)
                sm(*prev(c, j, 1))
                pv(*prev(c, j, 2))
            return carry
        lax.fori_loop(1, n_chunks, body, 0)
    last = n_chunks - 1
    sm(last, cpc - 1)
    pv(last, cpc - 2)
    pv(last, cpc - 1)


def _mods_kernel(c_ref, w_ref, b_ref, o_ref):
    x = _silu(c_ref[...])
    x_hi = x.astype(BF16).astype(F32)
    xs = jnp.concatenate([x_hi, x - x_hi], axis=0).astype(BF16)
    w = w_ref[...]
    w_hi = w.astype(BF16)
    w_lo = (w - w_hi.astype(F32)).astype(BF16)
    r_hi = _dot(xs, w_hi)
    r_lo = _dot(xs, w_lo)
    o_ref[...] = r_hi[:8] + r_hi[8:] + r_lo[:8] + b_ref[...]


def _mods(c8, w_ada, b_ada):
    tn = 512
    n = 3 * D_MODEL
    return pl.pallas_call(
        _mods_kernel,
        out_shape=jax.ShapeDtypeStruct((DEPTH, 8, n), F32),
        grid=(DEPTH, n // tn),
        in_specs=[
            pl.BlockSpec((8, D_MODEL), lambda l, j: (0, 0)),
            pl.BlockSpec((None, D_MODEL, tn), lambda l, j: (l, 0, j)),
            pl.BlockSpec((None, 1, tn), lambda l, j: (l, 0, j)),
        ],
        out_specs=pl.BlockSpec((None, 8, tn), lambda l, j: (l, 0, j)),
        compiler_params=_cparams(2, 48),
        name="mods",
    )(c8, w_ada, b_ada.reshape(DEPTH, 1, n))


def _norm_kernel(*refs, modulate):
    if modulate:
        x_ref, g_ref, mod_ref, o_ref = refs
    else:
        x_ref, g_ref, o_ref = refs
    gain = g_ref[...]
    if modulate:
        gain = gain * (1.0 + mod_ref[:, D_MODEL:2 * D_MODEL])

    def body(r, carry):
        rows = pl.ds(pl.multiple_of(r * NORM_ROWS, NORM_ROWS), NORM_ROWS)
        y = _rms(x_ref[rows, :], gain)
        if modulate:
            y = y + mod_ref[:, 0:D_MODEL]
        o_ref[rows, :] = y.astype(o_ref.dtype)
        return carry

    lax.fori_loop(0, NORM_TM // NORM_ROWS, body, 0, unroll=4)


def _norm(x, g, mods4=None, layer=0, row_fn=None, out_dtype=BF16):
    m = x.shape[0]
    modulate = mods4 is not None
    in_specs = [pl.BlockSpec((NORM_TM, D_MODEL), lambda i: (i, 0)),
                pl.BlockSpec((1, D_MODEL), lambda i: (0, 0))]
    args = [x, g]
    if modulate:
        in_specs.append(pl.BlockSpec((None, None, 1, 3 * D_MODEL),
                                     lambda i: (layer, row_fn(i * NORM_TM), 0, 0)))
        args.append(mods4)
    return pl.pallas_call(
        functools.partial(_norm_kernel, modulate=modulate),
        out_shape=jax.ShapeDtypeStruct((m, D_MODEL), out_dtype),
        grid=(m // NORM_TM,),
        in_specs=in_specs,
        out_specs=pl.BlockSpec((NORM_TM, D_MODEL), lambda i: (i, 0)),
        compiler_params=_cparams(1, 32),
        name="norm_mod" if modulate else "norm_final",
    )(*args)


Q_KINDS = ("aq", "cq", "dq")
KV_KINDS = ("plain", "dk", "ckv")
SCALES = (1.0, QS_A, QS_D)
SLICE_PARAMS = {
    "aq": ((0, 0, 1, 0),) * 4,
    "cq": ((1, 1, 1, 0),) * 4,
    "dq": ((0, 2, 2, 0),) * 4,
    "plain": ((0, 0, 0, 0),) * 4,
    "dk": ((0, 2, 0, 0),) * 4,
    "ckv": ((2, 1, 0, 0),) * 2 + ((0, 0, 0, 0),) * 2,
    "gate": ((0, 0, 0, 1),) * 4,
}
N_SLICES = TN // LANE
ROPE_HALF = (0, 32, 16)


def _proj_kernel(tbl_ref, *refs, lat, n_tiles, n_out):
    if lat:
        h_ref, w_ref, gains_ref, tab_ref = refs[:4]
        outs = refs[4:4 + n_out]
    else:
        h_ref, w_ref, gains_ref = refs[:3]
        outs = refs[3:3 + n_out]
    acc_refs = refs[-2:]
    s = pl.program_id(0)
    jm = s % n_tiles
    par0 = n_out + 1

    @pl.when(s == 0)
    def _():
        acc_refs[1][...] = jnp.zeros_like(acc_refs[1])

    def step(acc_new, acc_prev):
        acc_new[...] = _dot_nt(h_ref[...], w_ref[...].astype(BF16))
        for k in range(N_SLICES):
            gain_id, rope_id, scale_id, is_gate = (tbl_ref[par0 + 4 * k + f, jm] for f in range(4))
            sl = slice(k * LANE, (k + 1) * LANE)
            z = acc_prev[:, sl]
            r = jnp.where(gain_id != 0, _rms(z, gains_ref[gain_id]), z)
            if lat:
                half = jnp.where(rope_id == 2, ROPE_HALF[2], ROPE_HALF[1])
                r = _rope(r, tab_ref[3 * rope_id], tab_ref[3 * rope_id + 1], tab_ref[3 * rope_id + 2], half)
            scale = jnp.where(scale_id == 1, SCALES[1], jnp.where(scale_id == 2, SCALES[2], SCALES[0]))
            r = jnp.where(is_gate != 0, _silu(z), r * scale)
            for out_ref in outs:
                out_ref[:, sl] = r.astype(out_ref.dtype)

    for parity in range(2):
        @pl.when(s % 2 == parity)
        def _(parity=parity):
            step(acc_refs[parity], acc_refs[1 - parity])


ROW_UNIT = 64


def _wt_spec(rows, index_map):
    return pl.BlockSpec((None, pl.Element(rows), pl.Element(D_MODEL)), index_map)


def _proj_table(tiles, groups):
    n_tiles = len(tiles)
    kinds = [kd for kd, _ in tiles]
    rows = []
    for group in groups:
        writes = [t for t, kd in enumerate(kinds) if kd in group]
        pos = {t: n for n, t in enumerate(writes)}
        spare = lambda t: len(writes) + (t > writes[0])
        rows.append([pos.get((jm - 1) % n_tiles, spare((jm - 1) % n_tiles)) for jm in range(n_tiles)])
    assert all(r % ROW_UNIT == 0 for _, r in tiles)
    rows.append([r // ROW_UNIT for _, r in tiles])
    for k in range(N_SLICES):
        for f in range(4):
            rows.append([SLICE_PARAMS[kinds[(jm - 1) % n_tiles]][k][f] for jm in range(n_tiles)])
    return np.asarray(rows, np.int32)


def _proj(h, w_t, layer, tiles, gains, rope_tab, lat, name):
    m = h.shape[0]
    n_i = m // TM
    n_tiles = len(tiles)
    kinds = tuple(kd for kd, _ in tiles)
    groups = [Q_KINDS + KV_KINDS, ("gate",)] if lat else [Q_KINDS, KV_KINDS, ("gate",)]
    n_out = len(groups)
    tbl = _proj_table(tiles, groups)
    counts = [sum(kd in grp for kd in kinds) for grp in groups]

    def i_ep(s):
        return jnp.maximum(s - 1, 0) // n_tiles

    in_specs = [
        pl.BlockSpec((TM, D_MODEL), lambda s, t: (jnp.minimum(s // n_tiles, n_i - 1), 0)),
        _wt_spec(TN, lambda s, t: (layer, t[n_out, s % n_tiles] * ROW_UNIT, 0)),
        pl.BlockSpec((3, 1, LANE), lambda s, t: (0, 0, 0)),
    ]
    args = [h, w_t, gains]
    if lat:
        t_blocks = rope_tab.shape[1] // TM
        in_specs.append(pl.BlockSpec((9, TM, LANE), lambda s, t: (0, i_ep(s) % t_blocks, 0)))
        args.append(rope_tab)
    dtypes = [BF16, F32] if lat else [BF16, F32, F32]
    out_shape = tuple(jax.ShapeDtypeStruct((m, (n + 2) * TN), dt) for n, dt in zip(counts, dtypes))
    out_specs = tuple(
        pl.BlockSpec((TM, TN), lambda s, t, r=r, spare=counts[r]: (
            i_ep(s), jnp.where(s == 0, spare, t[r, s % n_tiles])))
        for r in range(n_out))
    return pl.pallas_call(
        functools.partial(_proj_kernel, lat=lat, n_tiles=n_tiles, n_out=n_out),
        out_shape=out_shape,
        grid_spec=pltpu.PrefetchScalarGridSpec(
            num_scalar_prefetch=1,
            grid=(n_i * n_tiles + 1,),
            in_specs=in_specs,
            out_specs=out_specs,
            scratch_shapes=[pltpu.VMEM((TM, TN), F32), pltpu.VMEM((TM, TN), F32)]),
        compiler_params=_cparams(1, 60),
        name=name,
    )(jnp.asarray(tbl), *args)


COL = {}
_off = 0
for _name, _size in (("a_q", 1024), ("a_k", 1024), ("a_v", 1024), ("a_z", 1024),
                     ("b_cq", B_Q_LORA), ("b_ckv", B_KV_LORA), ("b_kpe", B_ROPE), ("b_z", 1024),
                     ("c_q", 1024), ("c_k", 256), ("c_v", 256), ("c_z", 1024),
                     ("d_q", 1024), ("d_k", 1024), ("d_v", 1024), ("d_z", 1024)):
    COL[_name] = _off
    _off += _size


def _seg_tiles(kind, name, n_tiles):
    return [(kind, COL[name] + n * TN) for n in range(n_tiles)]


MAIN_TILES = (_seg_tiles("aq", "a_q", 2) + _seg_tiles("cq", "c_q", 2) + _seg_tiles("dq", "d_q", 2)
              + _seg_tiles("plain", "a_k", 2) + _seg_tiles("plain", "a_v", 2)
              + _seg_tiles("dk", "d_k", 2) + _seg_tiles("plain", "d_v", 2) + _seg_tiles("ckv", "c_k", 1)
              + _seg_tiles("gate", "a_z", 2) + _seg_tiles("gate", "b_z", 2)
              + _seg_tiles("gate", "c_z", 2) + _seg_tiles("gate", "d_z", 2))


def _bq_kernel(*refs, lat):
    if lat:
        h_ref, w_ref, g_ref, wuq_ref, cosd, slod, shid, o_ref, wb_ref = refs
    else:
        h_ref, w_ref, g_ref, wuq_ref, o_ref, wb_ref = refs

    @pl.when(pl.program_id(0) == 0)
    def _():
        wb_ref[...] = w_ref[...].astype(BF16)

    z = _dot_nt(h_ref[...], wb_ref[...])
    cq = _rms(z, g_ref[...]).astype(BF16)
    q = _dot(cq, wuq_ref[...])
    for hh in range(N_HEADS):
        c0 = hh * 2 * LANE
        o_ref[:, c0:c0 + LANE] = (q[:, c0:c0 + LANE] * QS_B).astype(BF16)
        pe = q[:, c0 + LANE:c0 + 2 * LANE]
        if lat:
            pe = _rope(pe, cosd[...], slod[...], shid[...], 16)
        o_ref[:, c0 + LANE:c0 + 2 * LANE] = (pe * QS_B).astype(BF16)


def _bq_proj(h, w_t, layer, gain, w_uq, tables_d, lat):
    m = h.shape[0]
    tm = TM_B
    in_specs = [
        pl.BlockSpec((tm, D_MODEL), lambda i: (i, 0)),
        _wt_spec(B_Q_LORA, lambda i: (layer, COL["b_cq"], 0)),
        pl.BlockSpec((1, B_Q_LORA), lambda i: (0, 0)),
        pl.BlockSpec((B_Q_LORA, N_HEADS * 2 * LANE), lambda i: (0, 0)),
    ]
    args = [h, w_t, gain, w_uq]
    if lat:
        t_blocks = tables_d[0].shape[0] // tm
        for t in tables_d:
            in_specs.append(pl.BlockSpec((tm, LANE), lambda i: (i % t_blocks, 0)))
            args.append(t)
    return pl.pallas_call(
        functools.partial(_bq_kernel, lat=lat),
        out_shape=jax.ShapeDtypeStruct((m, N_HEADS * 2 * LANE), BF16),
        grid=(m // tm,),
        in_specs=in_specs,
        out_specs=pl.BlockSpec((tm, N_HEADS * 2 * LANE), lambda i: (i, 0)),
        scratch_shapes=[pltpu.VMEM((B_Q_LORA, D_MODEL), BF16)],
        compiler_params=_cparams(1, 56),
        name="bq_proj_lat" if lat else "bq_proj_ctx",
    )(*args)


def _expand_kv(ckv, kpe, wukv_ref, kb_ref, vb_ref):
    kv = _dot(ckv.astype(BF16), wukv_ref[...])
    kpe_b = kpe.astype(BF16)
    for hh in range(N_HEADS):
        c0 = hh * 2 * LANE
        kb_ref[:, c0:c0 + LANE] = kv[:, c0:c0 + LANE].astype(BF16)
        kb_ref[:, c0 + LANE:c0 + 2 * LANE] = kpe_b
        vb_ref[:, hh * LANE:(hh + 1) * LANE] = kv[:, c0 + LANE:c0 + 2 * LANE].astype(BF16)


def _bkv_kernel(*refs, lat):
    if lat:
        h_ref, w_ref, wpe_ref, g_ref, wukv_ref, cosd, slod, shid, kb_ref, vb_ref, wb_ref, wpeb_ref = refs
    else:
        (h_ref, w_ref, wpe_ref, g_ref, wukv_ref, kb_ref, vb_ref, ckv_ref, kpe_ref,
         wb_ref, wpeb_ref) = refs

    @pl.when(pl.program_id(0) == 0)
    def _():
        wb_ref[...] = w_ref[...].astype(BF16)
        wpeb_ref[...] = wpe_ref[...].astype(BF16)

    h = h_ref[...]
    ckv = _rms(_dot_nt(h, wb_ref[...]), g_ref[...])
    kpe = _dot_nt(h, wpeb_ref[...])
    lane = lax.broadcasted_iota(jnp.int32, (1, LANE), 1)
    kpe = jnp.where(lane < B_ROPE, kpe, 0.0)
    if lat:
        kpe = _rope(kpe, cosd[...], slod[...], shid[...], 16)
    else:
        ckv_ref[...] = ckv
        kpe_ref[...] = kpe[:, :B_ROPE]
    _expand_kv(ckv, kpe, wukv_ref, kb_ref, vb_ref)


def _bkv_proj(h, w_t, layer, gain, w_ukv, tables_d, lat):
    m = h.shape[0]
    tm = TM_B
    in_specs = [
        pl.BlockSpec((tm, D_MODEL), lambda i: (i, 0)),
        _wt_spec(B_KV_LORA, lambda i: (layer, COL["b_ckv"], 0)),
        _wt_spec(LANE, lambda i: (layer, COL["b_kpe"], 0)),
        pl.BlockSpec((1, B_KV_LORA), lambda i: (0, 0)),
        pl.BlockSpec((B_KV_LORA, N_HEADS * 2 * LANE), lambda i: (0, 0)),
    ]
    args = [h, w_t, w_t, gain, w_ukv]
    out_shape = [jax.ShapeDtypeStruct((m, N_HEADS * 2 * LANE), BF16),
                 jax.ShapeDtypeStruct((m, GROUP_W), BF16)]
    out_specs = [pl.BlockSpec((tm, N_HEADS * 2 * LANE), lambda i: (i, 0)),
                 pl.BlockSpec((tm, GROUP_W), lambda i: (i, 0))]
    if lat:
        t_blocks = tables_d[0].shape[0] // tm
        for t in tables_d:
            in_specs.append(pl.BlockSpec((tm, LANE), lambda i: (i % t_blocks, 0)))
            args.append(t)
    else:
        out_shape += [jax.ShapeDtypeStruct((m, B_KV_LORA), F32),
                      jax.ShapeDtypeStruct((m, B_ROPE), F32)]
        out_specs += [pl.BlockSpec((tm, B_KV_LORA), lambda i: (i, 0)),
                      pl.BlockSpec((tm, B_ROPE), lambda i: (i, 0))]
    return pl.pallas_call(
        functools.partial(_bkv_kernel, lat=lat),
        out_shape=tuple(out_shape),
        grid=(m // tm,),
        in_specs=in_specs,
        out_specs=tuple(out_specs),
        scratch_shapes=[pltpu.VMEM((B_KV_LORA, D_MODEL), BF16), pltpu.VMEM((LANE, D_MODEL), BF16)],
        compiler_params=_cparams(1, 56),
        name="bkv_proj_lat" if lat else "bkv_proj_ctx",
    )(*args)


def _cache_kv_kernel(ckv_ref, kpe_ref, wukv_ref, kb_ref, vb_ref):
    _expand_kv(ckv_ref[...], kpe_ref[...], wukv_ref, kb_ref, vb_ref)


def _cache_kv(cache_ckv, cache_kpe128, w_ukv, layer):
    nb, _, s, _ = cache_ckv.shape
    return pl.pallas_call(
        _cache_kv_kernel,
        out_shape=(jax.ShapeDtypeStruct((nb * s, N_HEADS * 2 * LANE), BF16),
                   jax.ShapeDtypeStruct((nb * s, GROUP_W), BF16)),
        grid=(nb,),
        in_specs=[
            pl.BlockSpec((None, None, s, B_KV_LORA), lambda b: (b, layer, 0, 0)),
            pl.BlockSpec((None, None, s, LANE), lambda b: (b, layer, 0, 0)),
            pl.BlockSpec((B_KV_LORA, N_HEADS * 2 * LANE), lambda b: (0, 0)),
        ],
        out_specs=(pl.BlockSpec((s, N_HEADS * 2 * LANE), lambda b: (b, 0)),
                   pl.BlockSpec((s, GROUP_W), lambda b: (b, 0))),
        compiler_params=_cparams(1, 32),
        name="cache_kv",
    )(cache_ckv, cache_kpe128, w_ukv)


def _attn_kernel(*refs, n_src, nh, group, dk, n_chunks, diff, lam_init, k_staged):
    q_ref = refs[0]
    kv_refs = refs[1:1 + 2 * n_src]
    g_ref = refs[1 + 2 * n_src]
    pos = 2 + 2 * n_src
    if diff:
        lq1, lk1, lq2, lk2, subln = refs[pos:pos + 5]
        pos += 5
        lam = (jnp.exp(jnp.sum(lq1[...] * lk1[...], axis=-1, keepdims=True))
               - jnp.exp(jnp.sum(lq2[...] * lk2[...], axis=-1, keepdims=True)) + lam_init)
        lane = lax.broadcasted_iota(jnp.int32, (1, LANE), 1)
        m_lo = jnp.where(lane < D_QK, 1.0, 0.0).astype(BF16)
        m_hi = jnp.where(lane >= D_QK, 1.0, 0.0).astype(BF16)
    o_ref = refs[pos]
    scratch = list(refs[pos + 1:])
    n_kvh = nh // group

    k_refs, vt_refs = [], []
    for s in range(n_src):
        k_ref, v_ref = kv_refs[2 * s], kv_refs[2 * s + 1]
        if k_staged[s]:
            kb_ref = scratch.pop(0)
            kb_ref[...] = k_ref[...].astype(BF16)
            k_ref = kb_ref
        vt_ref = scratch.pop(0)
        for kh in range(n_kvh):
            v = v_ref[:, kh * LANE:(kh + 1) * LANE].astype(F32)
            vt_ref[kh * LANE:(kh + 1) * LANE, :] = v.T.astype(BF16)
        k_refs.append(k_ref)
        vt_refs.append(vt_ref)

    s_scr, p_scr, m_scr, l_scr = scratch[:4]
    n_w = 2 if diff else 1

    def rows_of(c):
        if isinstance(c, int):
            return slice(c * Q_CHUNK, (c + 1) * Q_CHUNK)
        return pl.ds(pl.multiple_of(c * Q_CHUNK, Q_CHUNK), Q_CHUNK)

    def qk(c, j):
        hh, w = divmod(j, n_w)
        kh = hh // group
        q = q_ref[rows_of(c), hh * dk:(hh + 1) * dk]
        if diff:
            q = q * (m_lo if w == 0 else m_hi)
        ks = [r[:, kh * dk:(kh + 1) * dk] for r in k_refs]
        _stage_qk(q, ks, [None] * n_src, s_scr.at[j % 2], m_scr.at[j % 2])

    def sm(c, j):
        _stage_softmax(s_scr.at[j % 2], m_scr.at[j % 2], p_scr.at[j % 2], l_scr.at[j % 2])

    def pv(c, j):
        hh, w = divmod(j, n_w)
        kh = hh // group
        vts = [r[kh * LANE:(kh + 1) * LANE, :] for r in vt_refs]
        o_t = _stage_pv(vts, p_scr.at[j % 2], l_scr.at[j % 2])
        if diff:
            o1_scr = scratch[4]
            if w == 0:
                o1_scr[...] = o_t
                return
            o = _rms((o1_scr[...] - lam * o_t).T, subln[...]) * (1.0 - lam_init)
        else:
            o = o_t.T
        rows = rows_of(c)
        osl = slice(hh * LANE, (hh + 1) * LANE)
        o_ref[rows, osl] = (o * g_ref[rows, osl]).astype(BF16)

    _pipeline(n_chunks, nh * n_w, qk, sm, pv)


def _attn(q, q_spec, srcs, g, g_spec, out_rows, out_spec, grid, *, nh, group, dk, t_q,
          diff_params=None, lam_init=0.0, name):
    in_specs = [q_spec]
    args = [q]
    scratch_shapes = []
    k_staged = []
    for ka, kspec, va, vspec in srcs:
        in_specs += [kspec, vspec]
        args += [ka, va]
        s_len, k_w = kspec.block_shape[-2:]
        v_w = vspec.block_shape[-1]
        k_staged.append(ka.dtype != BF16)
        if k_staged[-1]:
            scratch_shapes.append(pltpu.VMEM((s_len, k_w), BF16))
        scratch_shapes.append(pltpu.VMEM((v_w, s_len), BF16))
    in_specs.append(g_spec)
    args.append(g)
    s_total = sum(spec.block_shape[-2] for _, spec, _, _ in srcs)
    scratch_shapes += [pltpu.VMEM((2, s_total, Q_CHUNK), F32), pltpu.VMEM((2, s_total, Q_CHUNK), BF16),
                       pltpu.VMEM((2, 1, Q_CHUNK), F32), pltpu.VMEM((2, 1, Q_CHUNK), F32)]
    diff = diff_params is not None
    if diff:
        scratch_shapes.append(pltpu.VMEM((LANE, Q_CHUNK), F32))
        for p in diff_params:
            in_specs.append(pl.BlockSpec(p.shape, lambda *idx: (0, 0)))
            args.append(p)
    return pl.pallas_call(
        functools.partial(_attn_kernel, n_src=len(srcs), nh=nh, group=group, dk=dk,
                          n_chunks=t_q // Q_CHUNK, diff=diff, lam_init=lam_init,
                          k_staged=tuple(k_staged)),
        out_shape=jax.ShapeDtypeStruct((out_rows, GROUP_W), BF16),
        grid=grid,
        in_specs=in_specs,
        out_specs=out_spec,
        scratch_shapes=scratch_shapes,
        compiler_params=_cparams(len(grid), 48),
        name=name,
    )(*args)


NAT_ROWS_PER_CHUNK = Q_CHUNK // GRID_W
NAT_WIN_CHUNKS = 3
NAT_N_DR = 2 * A_KH - 1
NAT_N_DC = 2 * A_KW - 1


def _natten_kernel(rpb_ref, q_ref, k_ref, v_ref, kc_ref, vc_ref, g_ref, o_ref,
                   tl_ref, tr_ref, vt_ref, kcb_ref, vct_ref, s_scr, p_scr, m_scr, l_scr, *, n_rows):
    h = pl.program_id(0)
    b = pl.program_id(1)
    n_chunks = q_ref.shape[0] // Q_CHUNK

    @pl.when(b == 0)
    def _build():
        ck = lax.broadcasted_iota(jnp.int32, (GRID_W, LANE), 0)
        lane = lax.broadcasted_iota(jnp.int32, (GRID_W, LANE), 1)
        left = lane < GRID_W
        cq = jnp.where(left, lane, lane - GRID_W)
        delta = jnp.clip(ck - cq, -(A_KW - 1), A_KW - 1) + (A_KW - 1)
        start_c = jnp.clip(cq - A_KW // 2, 0, GRID_W - A_KW)
        col_ok = (ck >= start_c) & (ck < start_c + A_KW)
        for d in range(NAT_N_DR):
            t = jnp.zeros((GRID_W, LANE), F32)
            for e in range(NAT_N_DC):
                t = jnp.where(delta == e, rpb_ref[h * (NAT_N_DR * NAT_N_DC) + d * NAT_N_DC + e], t)
            t = jnp.where(col_ok, t * LOG2E, MASK_NEG)
            tl_ref[d] = jnp.where(left, t, 0.0)
            tr_ref[d] = jnp.where(left, 0.0, t)
        tl_ref[NAT_N_DR] = jnp.where(left, MASK_NEG, 0.0)
        tr_ref[NAT_N_DR] = jnp.where(left, 0.0, MASK_NEG)

    for j in range(n_chunks):
        vt_ref[j] = v_ref[j * Q_CHUNK:(j + 1) * Q_CHUNK, :].astype(F32).T.astype(BF16)
    kcb_ref[...] = kc_ref[...].astype(BF16)
    vct_ref[...] = vc_ref[...].T.astype(BF16)
    kh = min(A_KH, n_rows)

    def chunk_of(ci, j):
        return 2 * ci + j

    def rows_of(c):
        if isinstance(c, int):
            return slice(c * Q_CHUNK, (c + 1) * Q_CHUNK)
        return pl.ds(pl.multiple_of(c * Q_CHUNK, Q_CHUNK), Q_CHUNK)

    def window(c):
        return jnp.clip(c - 1, 0, n_chunks - NAT_WIN_CHUNKS)

    def qk(ci, j):
        c = chunk_of(ci, j)
        r0 = c * NAT_ROWS_PER_CHUNK
        w0 = window(c)
        ws = w0 * NAT_ROWS_PER_CHUNK
        ks, biases = [], []
        for w in range(NAT_WIN_CHUNKS):
            ks.append(k_ref[rows_of(w0 + w), :])
            blk_rows = []
            for jk in range(NAT_ROWS_PER_CHUNK):
                rk = ws + w * NAT_ROWS_PER_CHUNK + jk
                blks = []
                for qp in range(NAT_ROWS_PER_CHUNK // 2):
                    idx = []
                    for half in range(2):
                        rq = r0 + 2 * qp + half
                        st = jnp.clip(rq - kh // 2, 0, n_rows - kh)
                        valid = (rk >= st) & (rk < st + kh)
                        idx.append(jnp.where(valid, rk - rq + (A_KH - 1), NAT_N_DR))
                    blks.append(tl_ref[idx[0]] + tr_ref[idx[1]])
                blk_rows.append(jnp.concatenate(blks, axis=1))
            biases.append(jnp.concatenate(blk_rows, axis=0))
        ks.append(kcb_ref[...])
        biases.append(None)
        _stage_qk(q_ref[rows_of(c), :], ks, biases, s_scr.at[j], m_scr.at[j])

    def sm(ci, j):
        _stage_softmax(s_scr.at[j], m_scr.at[j], p_scr.at[j], l_scr.at[j])

    def pv(ci, j):
        c = chunk_of(ci, j)
        w0 = window(c)
        vts = [vt_ref[w0 + w] for w in range(NAT_WIN_CHUNKS)] + [vct_ref[...]]
        o = _stage_pv(vts, p_scr.at[j], l_scr.at[j]).T
        rows = rows_of(c)
        o_ref[rows, :] = (o * g_ref[rows, :]).astype(BF16)

    _pipeline(n_chunks // 2, 2, qk, sm, pv)


def _natten(rpb_flat, zb, cache_k, cache_v, g, layer, nb, t):
    s_ctx = cache_k.shape[2]
    n_rows = t // GRID_W
    return pl.pallas_call(
        functools.partial(_natten_kernel, n_rows=n_rows),
        out_shape=jax.ShapeDtypeStruct((nb * t, GROUP_W), BF16),
        grid=(N_HEADS, nb),
        in_specs=[
            pl.BlockSpec(memory_space=pltpu.SMEM),
            pl.BlockSpec((t, LANE), lambda h, b: (b, h)),
            pl.BlockSpec((t, LANE), lambda h, b: (b, 24 + h)),
            pl.BlockSpec((t, LANE), lambda h, b: (b, 32 + h)),
            pl.BlockSpec((None, None, s_ctx, LANE), lambda h, b: (b, layer, 0, h)),
            pl.BlockSpec((None, None, s_ctx, LANE), lambda h, b: (b, layer, 0, h)),
            pl.BlockSpec((t, LANE), lambda h, b: (b, h)),
        ],
        out_specs=pl.BlockSpec((t, LANE), lambda h, b: (b, h)),
        scratch_shapes=[pltpu.VMEM((NAT_N_DR + 1, GRID_W, LANE), F32),
                        pltpu.VMEM((NAT_N_DR + 1, GRID_W, LANE), F32),
                        pltpu.VMEM((t // Q_CHUNK, LANE, Q_CHUNK), BF16),
                        pltpu.VMEM((s_ctx, LANE), BF16),
                        pltpu.VMEM((LANE, s_ctx), BF16),
                        pltpu.VMEM((2, NAT_WIN_CHUNKS * Q_CHUNK + s_ctx, Q_CHUNK), F32),
                        pltpu.VMEM((2, NAT_WIN_CHUNKS * Q_CHUNK + s_ctx, Q_CHUNK), BF16),
                        pltpu.VMEM((2, 1, Q_CHUNK), F32),
                        pltpu.VMEM((2, 1, Q_CHUNK), F32)],
        compiler_params=_cparams(2, 48),
        name="natten",
    )(rpb_flat, zb, zb, zb, cache_k, cache_v, g)


def _outproj_kernel(ya, yb, yc, yd, w_ref, x_ref, gate_ref, o_ref):
    acc = None
    for n, y in enumerate((ya, yb, yc, yd)):
        part = _dot(y[...], w_ref[n * GROUP_W:(n + 1) * GROUP_W, :].astype(BF16))
        acc = part if acc is None else acc + part
    o_ref[...] = x_ref[...] + gate_ref[...] * acc


def _outproj(ys, w_out, x, mods4, layer, row_fn):
    m = x.shape[0]
    gate_blk0 = 2 * D_MODEL // TN
    y_spec = pl.BlockSpec((TM, GROUP_W), lambda i, j: (i, 0))
    return pl.pallas_call(
        _outproj_kernel,
        out_shape=jax.ShapeDtypeStruct((m, D_MODEL), F32),
        grid=(m // TM, D_MODEL // TN),
        in_specs=[y_spec, y_spec, y_spec, y_spec,
                  pl.BlockSpec((None, D_MODEL, TN), lambda i, j: (layer, 0, j)),
                  pl.BlockSpec((TM, TN), lambda i, j: (i, j)),
                  pl.BlockSpec((None, None, 1, TN),
                               lambda i, j: (layer, row_fn(i * TM), 0, gate_blk0 + j))],
        out_specs=pl.BlockSpec((TM, TN), lambda i, j: (i, j)),
        compiler_params=_cparams(2, 56),
        name="outproj",
    )(*ys, w_out, x, mods4)


def _axial_tables(t, head_dim):
    pos = np.arange(t)
    part = head_dim // 2
    half = part // 2
    freqs = ROPE_THETA ** (-(np.arange(half, dtype=np.float64) * 2.0 / part))
    lane = np.arange(LANE) % head_dim
    p = np.where(lane < part, (pos // GRID_W)[:, None], (pos % GRID_W)[:, None])
    within = lane % part
    ang = p * freqs[within % half][None, :]
    first = (within < half)[None, :]
    cos = np.cos(ang)
    sin = np.sin(ang)
    return (jnp.asarray(cos, F32), jnp.asarray(np.where(first, -sin, 0.0), F32),
            jnp.asarray(np.where(first, 0.0, sin), F32))


def _ctx_layer(x, l, lam_init, w_t, w_out, w_uq, w_ukv, p, mods4, nb, t):
    row_fn = lambda r: 0
    h = _norm(x, p["norm_g"], mods4, l, row_fn)
    zq, zf, g = _proj(h, w_t, l, MAIN_TILES, p["gains"], None, False, "main_proj_ctx")
    qb = _bq_proj(h, w_t, l, p["b_q_norm"], w_uq, None, lat=False)
    kb, vb, ckv_state, kpe_state = _bkv_proj(h, w_t, l, p["b_kv_norm"], w_ukv, None, lat=False)

    grid = (nb,)
    wide = lambda blk: pl.BlockSpec((t, GROUP_W), lambda b: (b, blk))
    out_spec = pl.BlockSpec((t, GROUP_W), lambda b: (b, 0))
    m = nb * t
    ya = _attn(zq, wide(0), [(zf, wide(0), zf, wide(1))], g, wide(0), m, out_spec, grid,
               nh=N_HEADS, group=1, dk=LANE, t_q=t, name="attn_a_ctx")
    yb = _attn(qb, pl.BlockSpec((t, 2 * GROUP_W), lambda b: (b, 0)),
               [(kb, pl.BlockSpec((t, 2 * GROUP_W), lambda b: (b, 0)), vb, wide(0))],
               g, wide(1), m, out_spec, grid, nh=N_HEADS, group=1, dk=2 * LANE, t_q=t,
               name="attn_b_ctx")
    ckv_w = C_KV_HEADS * HEAD_DIM
    c_k_blk = 4 * GROUP_W // ckv_w
    yc = _attn(zq, wide(1),
               [(zf, pl.BlockSpec((t, ckv_w), lambda b: (b, c_k_blk)),
                 zf, pl.BlockSpec((t, ckv_w), lambda b: (b, c_k_blk + 1)))],
               g, wide(2), m, out_spec, grid, nh=N_HEADS, group=N_HEADS // C_KV_HEADS,
               dk=LANE, t_q=t, name="attn_c_ctx")
    yd = _attn(zq, wide(2), [(zf, wide(2), zf, wide(3))], g, wide(3), m, out_spec, grid,
               nh=N_HEADS, group=1, dk=LANE, t_q=t,
               diff_params=p["diff"], lam_init=lam_init, name="attn_d_ctx")
    x_new = _outproj((ya, yb, yc, yd), w_out, x, mods4, l, row_fn)
    states = dict(
        a_k=zf[:, 0:GROUP_W], a_v=zf[:, GROUP_W:2 * GROUP_W],
        d_k=zf[:, 2 * GROUP_W:3 * GROUP_W], d_v=zf[:, 3 * GROUP_W:4 * GROUP_W],
        c_k=zf[:, 4 * GROUP_W:4 * GROUP_W + ckv_w], c_v=zf[:, 4 * GROUP_W + ckv_w:4 * GROUP_W + 2 * ckv_w],
        b_ckv=ckv_state, b_kpe=kpe_state)
    return x_new, states


def _lat_layer(x, l, lam_init, w_t, w_out, w_uq, w_ukv, p, mods4, caches, rope_tab, tables_d, nb, t):
    row_fn = lambda r: 1 + r // t
    h = _norm(x, p["norm_g"], mods4, l, row_fn)
    zb, g = _proj(h, w_t, l, MAIN_TILES, p["gains"], rope_tab, True, "main_proj_lat")
    qb = _bq_proj(h, w_t, l, p["b_q_norm"], w_uq, tables_d, lat=True)
    kb, vb = _bkv_proj(h, w_t, l, p["b_kv_norm"], w_ukv, tables_d, lat=True)
    kbc, vbc = _cache_kv(caches["b_ckv"], caches["b_kpe"], w_ukv, l)
    s_ctx = caches["a_k"].shape[2]
    m = nb * t

    ya = _natten(p["rpb"], zb, caches["a_k"], caches["a_v"], g, l, nb, t)

    hps = LAT_HEADS_PER_STEP
    grid = (nb, N_HEADS // hps)
    col = lambda blk0, w=hps: pl.BlockSpec((t, w * LANE), lambda b, h: (b, blk0 // w + h))
    cache = lambda w=hps: pl.BlockSpec((None, None, s_ctx, w * LANE), lambda b, h: (b, l, 0, h))
    out_spec = col(0)
    yb = _attn(qb, col(0, 2 * hps),
               [(kb, col(0, 2 * hps), vb, col(0)),
                (kbc, pl.BlockSpec((s_ctx, 2 * hps * LANE), lambda b, h: (b, h)),
                 vbc, pl.BlockSpec((s_ctx, hps * LANE), lambda b, h: (b, h)))],
               g, col(8), m, out_spec, grid, nh=hps, group=1, dk=2 * LANE, t_q=t, name="attn_b_lat")
    yd = _attn(zb, col(16),
               [(zb, col(40), zb, col(48)),
                (caches["d_k"], cache(), caches["d_v"], cache())],
               g, col(24), m, out_spec, grid, nh=hps, group=1, dk=LANE, t_q=t,
               diff_params=p["diff"], lam_init=lam_init, name="attn_d_lat")
    grp = N_HEADS // C_KV_HEADS
    yc = _attn(zb, col(8, grp),
               [(zb, col(56, 1), zb, col(58, 1)),
                (caches["c_k"], cache(1), caches["c_v"], cache(1))],
               g, col(16, grp), m, col(0, grp), (nb, C_KV_HEADS), nh=grp, group=grp, dk=LANE, t_q=t,
               name="attn_c_lat")
    return _outproj((ya, yb, yc, yd), w_out, x, mods4, l, row_fn)


def kernel(x_prompt, x_sample, cache_a_k, cache_a_v, cache_b_ckv, cache_b_kpe, cache_c_k, cache_c_v, cache_d_k, cache_d_v, c, c_ctx, norm_g, w_ada, b_ada, w_in, w_out, a_rpb, b_q_norm, b_w_uq, b_kv_norm, b_w_ukv, c_q_norm, c_k_norm, d_lq1, d_lk1, d_lq2, d_lk2, d_subln, final_norm_g):
    nb_c, t_c, _ = x_prompt.shape
    nb_l, t_l, _ = x_sample.shape
    s_ctx = cache_a_k.shape[2]

    c8 = jnp.concatenate([c_ctx[None, :], c, jnp.zeros((8 - 1 - nb_l, D_MODEL), F32)], axis=0)
    mods = _mods(c8, w_ada, b_ada)
    mods4 = mods[:, :1 + nb_l].reshape(DEPTH, 1 + nb_l, 1, 3 * D_MODEL)

    w_t = jnp.swapaxes(w_in, 1, 2)
    w_uq = jnp.pad(b_w_uq.reshape(DEPTH, B_Q_LORA, N_HEADS, B_NOPE + B_ROPE),
                   ((0, 0), (0, 0), (0, 0), (0, 2 * LANE - B_NOPE - B_ROPE)))
    w_uq = w_uq.reshape(DEPTH, B_Q_LORA, N_HEADS * 2 * LANE).astype(BF16)
    w_ukv = b_w_ukv.astype(BF16)

    caches = dict(
        a_k=cache_a_k.reshape(nb_l, DEPTH, s_ctx, GROUP_W),
        a_v=cache_a_v.reshape(nb_l, DEPTH, s_ctx, GROUP_W),
        b_ckv=cache_b_ckv,
        b_kpe=jnp.pad(cache_b_kpe, ((0, 0), (0, 0), (0, 0), (0, LANE - B_ROPE))),
        c_k=cache_c_k.reshape(nb_l, DEPTH, s_ctx, C_KV_HEADS * HEAD_DIM),
        c_v=cache_c_v.reshape(nb_l, DEPTH, s_ctx, C_KV_HEADS * HEAD_DIM),
        d_k=cache_d_k.reshape(nb_l, DEPTH, s_ctx, GROUP_W),
        d_v=cache_d_v.reshape(nb_l, DEPTH, s_ctx, GROUP_W),
    )
    tables_c = _axial_tables(t_l, HEAD_DIM)
    tables_d = _axial_tables(t_l, D_QK)
    no_rope = (jnp.ones((t_l, LANE), F32), jnp.zeros((t_l, LANE), F32), jnp.zeros((t_l, LANE), F32))
    rope_tab = jnp.stack(no_rope + tables_c + tables_d)

    xp = x_prompt.reshape(nb_c * t_c, D_MODEL)
    xs = x_sample.reshape(nb_l * t_l, D_MODEL)
    ctx_states = []
    for l in range(DEPTH):
        lam_init = 0.8 - 0.6 * math.exp(-0.3 * l)
        p = dict(
            norm_g=norm_g[l][None, :],
            gains=jnp.stack([jnp.ones((1, LANE), F32), c_q_norm[l][None, :], c_k_norm[l][None, :]]),
            b_q_norm=b_q_norm[l][None, :], b_kv_norm=b_kv_norm[l][None, :],
            rpb=a_rpb[l].reshape(-1),
            diff=(d_lq1[l][None, :], d_lk1[l][None, :], d_lq2[l][None, :], d_lk2[l][None, :],
                  d_subln[l][None, :]),
        )
        xp, st = _ctx_layer(xp, l, lam_init, w_t, w_out, w_uq[l], w_ukv[l], p, mods4, nb_c, t_c)
        ctx_states.append(st)
        xs = _lat_layer(xs, l, lam_init, w_t, w_out, w_uq[l], w_ukv[l], p, mods4, caches,
                        rope_tab, tables_d, nb_l, t_l)

    fg = final_norm_g[None, :]
    y_prompt = _norm(xp, fg, out_dtype=F32).reshape(nb_c, t_c, D_MODEL)
    y_sample = _norm(xs, fg, out_dtype=F32).reshape(nb_l, t_l, D_MODEL)

    def stack(name, tail):
        return jnp.stack([s[name].reshape((nb_c, t_c) + tail) for s in ctx_states], axis=1)

    return (y_prompt, y_sample,
            stack("a_k", (N_HEADS, HEAD_DIM)), stack("a_v", (N_HEADS, HEAD_DIM)),
            stack("b_ckv", (B_KV_LORA,)), stack("b_kpe", (B_ROPE,)),
            stack("c_k", (C_KV_HEADS, HEAD_DIM)), stack("c_v", (C_KV_HEADS, HEAD_DIM)),
            stack("d_k", (N_HEADS, 2, D_QK)), stack("d_v", (N_HEADS, HEAD_DIM)))
```

```python
import functools
import math

import numpy as np
import jax
import jax.numpy as jnp
from jax import lax
from jax.experimental import pallas as pl
from jax.experimental.pallas import tpu as pltpu

F32 = jnp.float32
BF16 = jnp.bfloat16

D_MODEL = 4096
DEPTH = 2
GRID_W = 64
HEAD_DIM = 128
N_HEADS = 8
GROUP_W = N_HEADS * HEAD_DIM
A_KH = 8
A_KW = 16
B_NOPE = 128
B_ROPE = 64
B_Q_LORA = 768
B_KV_LORA = 512
C_KV_HEADS = 2
D_QK = 64
ROPE_THETA = 10000.0
EPS = 1e-6

LANE = 128
MASK_NEG = -1e30

LOG2E = math.log2(math.e)
QS_A = HEAD_DIM ** -0.5 * LOG2E
QS_B = (B_NOPE + B_ROPE) ** -0.5 * LOG2E
QS_D = D_QK ** -0.5 * LOG2E

TM = 1024
TN = 512
TM_B = 512
Q_CHUNK = 256
LAT_HEADS_PER_STEP = 2
LAT_B_HEADS_PER_STEP = 2
CHUNK_UNROLL = 2
NORM_TM = 512
NORM_ROWS = 16


def _cparams(n_axes, vmem_mib):
    return pltpu.CompilerParams(dimension_semantics=("arbitrary",) * n_axes,
                                vmem_limit_bytes=vmem_mib * 2 ** 20)


def _rms(z, g):
    ms = jnp.mean(z * z, axis=-1, keepdims=True)
    return z * lax.rsqrt(ms + EPS) * g


def _silu(z):
    return z * jax.nn.sigmoid(z)


def _rope(z, cos, sin_lo, sin_hi, half):
    return (z * cos + pltpu.roll(z, LANE - half, 1) * sin_lo
            + pltpu.roll(z, half, 1) * sin_hi)


def _dot(a, b):
    return jnp.dot(a, b, preferred_element_type=F32)


def _dot_nt(a, b):
    return lax.dot_general(a, b, (((1,), (1,)), ((), ())), preferred_element_type=F32)


def _stage_qk(q, ks, biases, s_ref, m_ref):
    off = 0
    m = None
    for k, bias in zip(ks, biases):
        st = _dot_nt(k, q)
        if bias is not None:
            st = st + bias
        s_ref[off:off + k.shape[0], :] = st
        mi = st.max(axis=0, keepdims=True)
        m = mi if m is None else jnp.maximum(m, mi)
        off += k.shape[0]
    m_ref[...] = m


def _stage_softmax(s_ref, m_ref, p_ref, l_ref):
    p = jnp.exp2(s_ref[...] - m_ref[...])
    l_ref[...] = p.sum(axis=0, keepdims=True)
    p_ref[...] = p.astype(BF16)


def _stage_pv(vts, p_ref, l_ref):
    off = 0
    o_t = None
    for vt in vts:
        part = _dot(vt, p_ref[off:off + vt.shape[1], :])
        o_t = part if o_t is None else o_t + part
        off += vt.shape[1]
    return o_t / l_ref[...]


def _pipeline(n_chunks, cpc, qk, sm, pv):
    assert cpc % 2 == 0

    def prev(c, j, back):
        return (c, j - back) if j >= back else (c - 1, j - back + cpc)

    for j in range(cpc):
        qk(0, j)
        if j >= 1:
            sm(0, j - 1)
        if j >= 2:
            pv(0, j - 2)
    if n_chunks > 1:
        def body(c, carry):
            for j in range(cpc):
                qk(c, j)
                sm(*prev(c, j, 1))
                pv(*prev(c, j, 2))
            return carry
        lax.fori_loop(1, n_chunks, body, 0)
    last = n_chunks - 1
    sm(last, cpc - 1)
    pv(last, cpc - 2)
    pv(last, cpc - 1)


def _mods_kernel(c_ref, w_ref, b_ref, o_ref):
    x = _silu(c_ref[...])
    x_hi = x.astype(BF16).astype(F32)
    xs = jnp.concatenate([x_hi, x - x_hi], axis=0).astype(BF16)
    w = w_ref[...]
    w_hi = w.astype(BF16)
    w_lo = (w - w_hi.astype(F32)).astype(BF16)
    r_hi = _dot(xs, w_hi)
    r_lo = _dot(xs, w_lo)
    o_ref[...] = r_hi[:8] + r_hi[8:] + r_lo[:8] + b_ref[...]


def _mods(c8, w_ada, b_ada):
    tn = 512
    n = 3 * D_MODEL
    return pl.pallas_call(
        _mods_kernel,
        out_shape=jax.ShapeDtypeStruct((DEPTH, 8, n), F32),
        grid=(DEPTH, n // tn),
        in_specs=[
            pl.BlockSpec((8, D_MODEL), lambda l, j: (0, 0)),
            pl.BlockSpec((None, D_MODEL, tn), lambda l, j: (l, 0, j)),
            pl.BlockSpec((None, 1, tn), lambda l, j: (l, 0, j)),
        ],
        out_specs=pl.BlockSpec((None, 8, tn), lambda l, j: (l, 0, j)),
        compiler_params=_cparams(2, 48),
        name="mods",
    )(c8, w_ada, b_ada.reshape(DEPTH, 1, n))


def _norm_kernel(*refs, modulate):
    if modulate:
        x_ref, g_ref, mod_ref, o_ref = refs
    else:
        x_ref, g_ref, o_ref = refs
    gain = g_ref[...]
    if modulate:
        gain = gain * (1.0 + mod_ref[:, D_MODEL:2 * D_MODEL])

    def body(r, carry):
        rows = pl.ds(pl.multiple_of(r * NORM_ROWS, NORM_ROWS), NORM_ROWS)
        y = _rms(x_ref[rows, :], gain)
        if modulate:
            y = y + mod_ref[:, 0:D_MODEL]
        o_ref[rows, :] = y.astype(o_ref.dtype)
        return carry

    lax.fori_loop(0, NORM_TM // NORM_ROWS, body, 0, unroll=4)


def _norm(x, g, mods4=None, layer=0, row_fn=None, out_dtype=BF16):
    m = x.shape[0]
    modulate = mods4 is not None
    in_specs = [pl.BlockSpec((NORM_TM, D_MODEL), lambda i: (i, 0)),
                pl.BlockSpec((1, D_MODEL), lambda i: (0, 0))]
    args = [x, g]
    if modulate:
        in_specs.append(pl.BlockSpec((None, None, 1, 3 * D_MODEL),
                                     lambda i: (layer, row_fn(i * NORM_TM), 0, 0)))
        args.append(mods4)
    return pl.pallas_call(
        functools.partial(_norm_kernel, modulate=modulate),
        out_shape=jax.ShapeDtypeStruct((m, D_MODEL), out_dtype),
        grid=(m // NORM_TM,),
        in_specs=in_specs,
        out_specs=pl.BlockSpec((NORM_TM, D_MODEL), lambda i: (i, 0)),
        compiler_params=_cparams(1, 48),
        name="norm_mod" if modulate else "norm_final",
    )(*args)


Q_KINDS = ("aq", "cq", "dq")
KV_KINDS = ("plain", "dk", "ckv")
SCALES = (1.0, QS_A, QS_D)
SLICE_PARAMS = {
    "aq": ((0, 0, 1, 0),) * 4,
    "cq": ((1, 1, 1, 0),) * 4,
    "dq": ((0, 2, 2, 0),) * 4,
    "plain": ((0, 0, 0, 0),) * 4,
    "dk": ((0, 2, 0, 0),) * 4,
    "ckv": ((2, 1, 0, 0),) * 2 + ((0, 0, 0, 0),) * 2,
    "gate": ((0, 0, 0, 1),) * 4,
}
N_SLICES = TN // LANE
ROPE_HALF = (0, 32, 16)


def _proj_kernel(tbl_ref, *refs, lat, n_tiles, n_out, n_in):
    h_ref, w_ref, gains_ref = refs[:3]
    tab_ref = refs[3] if lat else None
    outs = refs[n_in:n_in + n_out]
    acc_refs = refs[-2:]
    s = pl.program_id(0)
    jm = s % n_tiles
    par0 = n_out + 1

    @pl.when(s == 0)
    def _():
        acc_refs[1][...] = jnp.zeros_like(acc_refs[1])

    def step(acc_new, acc_prev):
        acc_new[...] = _dot_nt(h_ref[...], w_ref[...].astype(BF16))
        for k in range(N_SLICES):
            gain_id, rope_id, scale_id, is_gate = (tbl_ref[par0 + 4 * k + f, jm] for f in range(4))
            sl = slice(k * LANE, (k + 1) * LANE)
            z = acc_prev[:, sl]
            r = jnp.where(gain_id != 0, _rms(z, gains_ref[gain_id]), z)
            if lat:
                half = jnp.where(rope_id == 2, ROPE_HALF[2], ROPE_HALF[1])
                r = _rope(r, tab_ref[3 * rope_id], tab_ref[3 * rope_id + 1], tab_ref[3 * rope_id + 2], half)
            scale = jnp.where(scale_id == 1, SCALES[1], jnp.where(scale_id == 2, SCALES[2], SCALES[0]))
            r = jnp.where(is_gate != 0, _silu(z), r * scale)
            for out_ref in outs:
                if len(out_ref.shape) == 2:
                    out_ref[:, sl] = r.astype(out_ref.dtype)
                else:
                    out_ref[:, :, sl] = r.astype(out_ref.dtype).reshape(out_ref.shape[:2] + (LANE,))

    for parity in range(2):
        @pl.when(s % 2 == parity)
        def _(parity=parity):
            step(acc_refs[parity], acc_refs[1 - parity])


ROW_UNIT = 64


def _wt_spec(rows, index_map):
    return pl.BlockSpec((None, pl.Element(rows), pl.Element(D_MODEL)), index_map)


def _proj_table(tiles, groups):
    n_tiles = len(tiles)
    kinds = [kd for kd, _ in tiles]
    rows = []
    for group in groups:
        writes = [t for t, kd in enumerate(kinds) if kd in group]
        pos = {t: n for n, t in enumerate(writes)}
        spare = lambda t: len(writes) + (t > writes[0])
        rows.append([pos.get((jm - 1) % n_tiles, spare((jm - 1) % n_tiles)) for jm in range(n_tiles)])
    assert all(r % ROW_UNIT == 0 for _, r in tiles)
    rows.append([r // ROW_UNIT for _, r in tiles])
    for k in range(N_SLICES):
        for f in range(4):
            rows.append([SLICE_PARAMS[kinds[(jm - 1) % n_tiles]][k][f] for jm in range(n_tiles)])
    return np.asarray(rows, np.int32)


def _proj(h, w_t, layer, tiles, gains, rope_tab, lat, name, seq=None, kv_state=None):
    m = h.shape[0]
    n_i = m // TM
    n_tiles = len(tiles)
    kinds = tuple(kd for kd, _ in tiles)
    groups = [Q_KINDS + KV_KINDS, ("gate",)] if lat else [Q_KINDS, KV_KINDS, ("gate",)]
    n_out = len(groups)
    tbl = _proj_table(tiles, groups)
    counts = [sum(kd in grp for kd in kinds) for grp in groups]

    def i_ep(s):
        return jnp.maximum(s - 1, 0) // n_tiles

    in_specs = [
        pl.BlockSpec((TM, D_MODEL), lambda s, t: (jnp.minimum(s // n_tiles, n_i - 1), 0)),
        _wt_spec(TN, lambda s, t: (layer, t[n_out, s % n_tiles] * ROW_UNIT, 0)),
        pl.BlockSpec((3, 1, LANE), lambda s, t: (0, 0, 0)),
    ]
    args = [h, w_t, gains]
    if lat:
        t_blocks = rope_tab.shape[1] // TM
        in_specs.append(pl.BlockSpec((9, TM, LANE), lambda s, t: (0, i_ep(s) % t_blocks, 0)))
        args.append(rope_tab)
    dtypes = [BF16, F32] if lat else [BF16, F32, F32]
    out_shape = [jax.ShapeDtypeStruct((m, (n + 2) * TN), dt) for n, dt in zip(counts, dtypes)]
    col = lambda s, t, r: jnp.where(s == 0, counts[r], t[r, s % n_tiles])
    out_specs = [pl.BlockSpec((TM, TN), lambda s, t, r=r: (i_ep(s), col(s, t, r))) for r in range(n_out)]
    aliases = {}
    if not lat:
        out_shape[1] = jax.ShapeDtypeStruct((m // seq, DEPTH, seq, (counts[1] + 2) * TN), F32)
        out_specs[1] = pl.BlockSpec((TM // seq, None, seq, TN), lambda s, t: (i_ep(s), layer, 0, col(s, t, 1)))
        if kv_state is not None:
            in_specs.append(pl.BlockSpec(memory_space=pl.ANY))
            args.append(kv_state)
            aliases = {len(args): 1}
    return pl.pallas_call(
        functools.partial(_proj_kernel, lat=lat, n_tiles=n_tiles, n_out=n_out, n_in=len(args)),
        out_shape=tuple(out_shape),
        input_output_aliases=aliases,
        grid_spec=pltpu.PrefetchScalarGridSpec(
            num_scalar_prefetch=1,
            grid=(n_i * n_tiles + 1,),
            in_specs=in_specs,
            out_specs=tuple(out_specs),
            scratch_shapes=[pltpu.VMEM((TM, TN), F32), pltpu.VMEM((TM, TN), F32)]),
        compiler_params=_cparams(1, 60),
        name=name,
    )(jnp.asarray(tbl), *args)


COL = {}
_off = 0
for _name, _size in (("a_q", 1024), ("a_k", 1024), ("a_v", 1024), ("a_z", 1024),
                     ("b_cq", B_Q_LORA), ("b_ckv", B_KV_LORA), ("b_kpe", B_ROPE), ("b_z", 1024),
                     ("c_q", 1024), ("c_k", 256), ("c_v", 256), ("c_z", 1024),
                     ("d_q", 1024), ("d_k", 1024), ("d_v", 1024), ("d_z", 1024)):
    COL[_name] = _off
    _off += _size


def _seg_tiles(kind, name, n_tiles):
    return [(kind, COL[name] + n * TN) for n in range(n_tiles)]


MAIN_TILES = (_seg_tiles("aq", "a_q", 2) + _seg_tiles("cq", "c_q", 2) + _seg_tiles("dq", "d_q", 2)
              + _seg_tiles("plain", "a_k", 2) + _seg_tiles("plain", "a_v", 2)
              + _seg_tiles("dk", "d_k", 2) + _seg_tiles("plain", "d_v", 2) + _seg_tiles("ckv", "c_k", 1)
              + _seg_tiles("gate", "a_z", 2) + _seg_tiles("gate", "b_z", 2)
              + _seg_tiles("gate", "c_z", 2) + _seg_tiles("gate", "d_z", 2))


def _bq_kernel(*refs, lat):
    if lat:
        h_ref, w_ref, g_ref, wuq_ref, cosd, slod, shid, o_ref, wb_ref = refs
    else:
        h_ref, w_ref, g_ref, wuq_ref, o_ref, wb_ref = refs

    @pl.when(pl.program_id(0) == 0)
    def _():
        wb_ref[...] = w_ref[...].astype(BF16)

    z = _dot_nt(h_ref[...], wb_ref[...])
    cq = _rms(z, g_ref[...]).astype(BF16)
    q = _dot(cq, wuq_ref[...])
    for hh in range(N_HEADS):
        c0 = hh * 2 * LANE
        o_ref[:, c0:c0 + LANE] = (q[:, c0:c0 + LANE] * QS_B).astype(BF16)
        pe = q[:, c0 + LANE:c0 + 2 * LANE]
        if lat:
            pe = _rope(pe, cosd[...], slod[...], shid[...], 16)
        o_ref[:, c0 + LANE:c0 + 2 * LANE] = (pe * QS_B).astype(BF16)


def _bq_proj(h, w_t, layer, gain, w_uq, tables_d, lat):
    m = h.shape[0]
    tm = TM_B
    in_specs = [
        pl.BlockSpec((tm, D_MODEL), lambda i: (i, 0)),
        _wt_spec(B_Q_LORA, lambda i: (layer, COL["b_cq"], 0)),
        pl.BlockSpec((1, B_Q_LORA), lambda i: (0, 0)),
        pl.BlockSpec((B_Q_LORA, N_HEADS * 2 * LANE), lambda i: (0, 0)),
    ]
    args = [h, w_t, gain, w_uq]
    if lat:
        t_blocks = tables_d[0].shape[0] // tm
        for t in tables_d:
            in_specs.append(pl.BlockSpec((tm, LANE), lambda i: (i % t_blocks, 0)))
            args.append(t)
    return pl.pallas_call(
        functools.partial(_bq_kernel, lat=lat),
        out_shape=jax.ShapeDtypeStruct((m, N_HEADS * 2 * LANE), BF16),
        grid=(m // tm,),
        in_specs=in_specs,
        out_specs=pl.BlockSpec((tm, N_HEADS * 2 * LANE), lambda i: (i, 0)),
        scratch_shapes=[pltpu.VMEM((B_Q_LORA, D_MODEL), BF16)],
        compiler_params=_cparams(1, 56),
        name="bq_proj_lat" if lat else "bq_proj_ctx",
    )(*args)


def _expand_kv(ckv, kpe, wukv_ref, kb_ref, vb_ref):
    kv = _dot(ckv.astype(BF16), wukv_ref[...])
    kpe_b = kpe.astype(BF16)
    for hh in range(N_HEADS):
        c0 = hh * 2 * LANE
        kb_ref[:, c0:c0 + LANE] = kv[:, c0:c0 + LANE].astype(BF16)
        kb_ref[:, c0 + LANE:c0 + 2 * LANE] = kpe_b
        vb_ref[:, hh * LANE:(hh + 1) * LANE] = kv[:, c0 + LANE:c0 + 2 * LANE].astype(BF16)


def _bkv_kernel(*refs, lat):
    if lat:
        h_ref, w_ref, wpe_ref, g_ref, wukv_ref, cosd, slod, shid, kb_ref, vb_ref, wb_ref, wpeb_ref = refs
    else:
        h_ref, w_ref, wpe_ref, g_ref, wukv_ref = refs[:5]
        kb_ref, vb_ref, ckv_ref, kpe_ref, wb_ref, wpeb_ref = refs[-6:]

    @pl.when(pl.program_id(0) == 0)
    def _():
        wb_ref[...] = w_ref[...].astype(BF16)
        wpeb_ref[...] = wpe_ref[...].astype(BF16)

    h = h_ref[...]
    ckv = _rms(_dot_nt(h, wb_ref[...]), g_ref[...])
    kpe = _dot_nt(h, wpeb_ref[...])
    lane = lax.broadcasted_iota(jnp.int32, (1, LANE), 1)
    kpe = jnp.where(lane < B_ROPE, kpe, 0.0)
    if lat:
        kpe = _rope(kpe, cosd[...], slod[...], shid[...], 16)
    else:
        ckv_ref[...] = ckv.reshape(ckv_ref.shape)
        kpe_ref[...] = kpe[:, :B_ROPE].reshape(kpe_ref.shape)
    _expand_kv(ckv, kpe, wukv_ref, kb_ref, vb_ref)


def _bkv_proj(h, w_t, layer, gain, w_ukv, tables_d, lat, seq=None, states=None):
    m = h.shape[0]
    tm = TM_B
    in_specs = [
        pl.BlockSpec((tm, D_MODEL), lambda i: (i, 0)),
        _wt_spec(B_KV_LORA, lambda i: (layer, COL["b_ckv"], 0)),
        _wt_spec(LANE, lambda i: (layer, COL["b_kpe"], 0)),
        pl.BlockSpec((1, B_KV_LORA), lambda i: (0, 0)),
        pl.BlockSpec((B_KV_LORA, N_HEADS * 2 * LANE), lambda i: (0, 0)),
    ]
    args = [h, w_t, w_t, gain, w_ukv]
    out_shape = [jax.ShapeDtypeStruct((m, N_HEADS * 2 * LANE), BF16),
                 jax.ShapeDtypeStruct((m, GROUP_W), BF16)]
    out_specs = [pl.BlockSpec((tm, N_HEADS * 2 * LANE), lambda i: (i, 0)),
                 pl.BlockSpec((tm, GROUP_W), lambda i: (i, 0))]
    if lat:
        t_blocks = tables_d[0].shape[0] // tm
        for t in tables_d:
            in_specs.append(pl.BlockSpec((tm, LANE), lambda i: (i % t_blocks, 0)))
            args.append(t)
    aliases = {}
    if not lat:
        out_shape += [jax.ShapeDtypeStruct((m // seq, DEPTH, seq, B_KV_LORA), F32),
                      jax.ShapeDtypeStruct((m // seq, DEPTH, seq, B_ROPE), F32)]
        out_specs += [pl.BlockSpec((tm // seq, None, seq, B_KV_LORA), lambda i: (i, layer, 0, 0)),
                      pl.BlockSpec((tm // seq, None, seq, B_ROPE), lambda i: (i, layer, 0, 0))]
        if states is not None:
            for n, st in enumerate(states):
                in_specs.append(pl.BlockSpec(memory_space=pl.ANY))
                aliases[len(args)] = 2 + n
                args.append(st)
    return pl.pallas_call(
        functools.partial(_bkv_kernel, lat=lat),
        out_shape=tuple(out_shape),
        input_output_aliases=aliases,
        grid=(m // tm,),
        in_specs=in_specs,
        out_specs=tuple(out_specs),
        scratch_shapes=[pltpu.VMEM((B_KV_LORA, D_MODEL), BF16), pltpu.VMEM((LANE, D_MODEL), BF16)],
        compiler_params=_cparams(1, 56),
        name="bkv_proj_lat" if lat else "bkv_proj_ctx",
    )(*args)


def _cache_kv_kernel(ckv_ref, kpe_ref, wukv_ref, kb_ref, vb_ref):
    _expand_kv(ckv_ref[...], kpe_ref[...], wukv_ref, kb_ref, vb_ref)


def _cache_kv(cache_ckv, cache_kpe128, w_ukv, layer):
    nb, _, s, _ = cache_ckv.shape
    return pl.pallas_call(
        _cache_kv_kernel,
        out_shape=(jax.ShapeDtypeStruct((nb * s, N_HEADS * 2 * LANE), BF16),
                   jax.ShapeDtypeStruct((nb * s, GROUP_W), BF16)),
        grid=(nb,),
        in_specs=[
            pl.BlockSpec((None, None, s, B_KV_LORA), lambda b: (b, layer, 0, 0)),
            pl.BlockSpec((None, None, s, LANE), lambda b: (b, layer, 0, 0)),
            pl.BlockSpec((B_KV_LORA, N_HEADS * 2 * LANE), lambda b: (0, 0)),
        ],
        out_specs=(pl.BlockSpec((s, N_HEADS * 2 * LANE), lambda b: (b, 0)),
                   pl.BlockSpec((s, GROUP_W), lambda b: (b, 0))),
        compiler_params=_cparams(1, 32),
        name="cache_kv",
    )(cache_ckv, cache_kpe128, w_ukv)


def _attn_kernel(*refs, n_src, nh, group, dk, n_chunks, diff, lam_init, k_staged):
    q_ref = refs[0]
    kv_refs = refs[1:1 + 2 * n_src]
    g_ref = refs[1 + 2 * n_src]
    pos = 2 + 2 * n_src
    if diff:
        lq1, lk1, lq2, lk2, subln = refs[pos:pos + 5]
        pos += 5
        lam = (jnp.exp(jnp.sum(lq1[...] * lk1[...], axis=-1, keepdims=True))
               - jnp.exp(jnp.sum(lq2[...] * lk2[...], axis=-1, keepdims=True)) + lam_init)
        lane = lax.broadcasted_iota(jnp.int32, (1, LANE), 1)
        m_lo = jnp.where(lane < D_QK, 1.0, 0.0).astype(BF16)
        m_hi = jnp.where(lane >= D_QK, 1.0, 0.0).astype(BF16)
    o_ref = refs[pos]
    scratch = list(refs[pos + 1:])
    n_kvh = nh // group

    k_refs, vt_refs = [], []
    for s in range(n_src):
        k_ref, v_ref = kv_refs[2 * s], kv_refs[2 * s + 1]
        if k_staged[s]:
            kb_ref = scratch.pop(0)
            kb_ref[...] = k_ref[...].astype(BF16)
            k_ref = kb_ref
        vt_ref = scratch.pop(0)
        for kh in range(n_kvh):
            v = v_ref[:, kh * LANE:(kh + 1) * LANE].astype(F32)
            vt_ref[kh * LANE:(kh + 1) * LANE, :] = v.T.astype(BF16)
        k_refs.append(k_ref)
        vt_refs.append(vt_ref)

    s_scr, p_scr, m_scr, l_scr = scratch[:4]
    n_w = 2 if diff else 1

    def rows_of(c):
        if isinstance(c, int):
            return slice(c * Q_CHUNK, (c + 1) * Q_CHUNK)
        return pl.ds(pl.multiple_of(c * Q_CHUNK, Q_CHUNK), Q_CHUNK)

    unroll = CHUNK_UNROLL if n_chunks % CHUNK_UNROLL == 0 else 1

    def split(c, j):
        sub, jj = divmod(j, nh * n_w)
        return (c * unroll + sub,) + divmod(jj, n_w)

    def qk(c, j):
        c, hh, w = split(c, j)
        kh = hh // group
        q = q_ref[rows_of(c), hh * dk:(hh + 1) * dk]
        if diff:
            q = q * (m_lo if w == 0 else m_hi)
        ks = [r[:, kh * dk:(kh + 1) * dk] for r in k_refs]
        _stage_qk(q, ks, [None] * n_src, s_scr.at[j % 2], m_scr.at[j % 2])

    def sm(c, j):
        _stage_softmax(s_scr.at[j % 2], m_scr.at[j % 2], p_scr.at[j % 2], l_scr.at[j % 2])

    def pv(c, j):
        c, hh, w = split(c, j)
        kh = hh // group
        vts = [r[kh * LANE:(kh + 1) * LANE, :] for r in vt_refs]
        o_t = _stage_pv(vts, p_scr.at[j % 2], l_scr.at[j % 2])
        if diff:
            o1_scr = scratch[4]
            if w == 0:
                o1_scr[...] = o_t
                return
            o = _rms((o1_scr[...] - lam * o_t).T, subln[...]) * (1.0 - lam_init)
        else:
            o = o_t.T
        rows = rows_of(c)
        osl = slice(hh * LANE, (hh + 1) * LANE)
        o_ref[rows, osl] = (o * g_ref[rows, osl]).astype(BF16)

    _pipeline(n_chunks // unroll, nh * n_w * unroll, qk, sm, pv)


def _attn(q, q_spec, srcs, g, g_spec, out_rows, out_spec, grid, *, nh, group, dk, t_q,
          diff_params=None, lam_init=0.0, name):
    in_specs = [q_spec]
    args = [q]
    scratch_shapes = []
    k_staged = []
    for ka, kspec, va, vspec in srcs:
        in_specs += [kspec, vspec]
        args += [ka, va]
        s_len, k_w = kspec.block_shape[-2:]
        v_w = vspec.block_shape[-1]
        k_staged.append(ka.dtype != BF16)
        if k_staged[-1]:
            scratch_shapes.append(pltpu.VMEM((s_len, k_w), BF16))
        scratch_shapes.append(pltpu.VMEM((v_w, s_len), BF16))
    in_specs.append(g_spec)
    args.append(g)
    s_total = sum(spec.block_shape[-2] for _, spec, _, _ in srcs)
    scratch_shapes += [pltpu.VMEM((2, s_total, Q_CHUNK), F32), pltpu.VMEM((2, s_total, Q_CHUNK), BF16),
                       pltpu.VMEM((2, 1, Q_CHUNK), F32), pltpu.VMEM((2, 1, Q_CHUNK), F32)]
    diff = diff_params is not None
    if diff:
        scratch_shapes.append(pltpu.VMEM((LANE, Q_CHUNK), F32))
        for p in diff_params:
            in_specs.append(pl.BlockSpec(p.shape, lambda *idx: (0, 0)))
            args.append(p)
    return pl.pallas_call(
        functools.partial(_attn_kernel, n_src=len(srcs), nh=nh, group=group, dk=dk,
                          n_chunks=t_q // Q_CHUNK, diff=diff, lam_init=lam_init,
                          k_staged=tuple(k_staged)),
        out_shape=jax.ShapeDtypeStruct((out_rows, GROUP_W), BF16),
        grid=grid,
        in_specs=in_specs,
        out_specs=out_spec,
        scratch_shapes=scratch_shapes,
        compiler_params=_cparams(len(grid), 56),
        name=name,
    )(*args)


NAT_ROWS_PER_CHUNK = Q_CHUNK // GRID_W
NAT_WIN_CHUNKS = 3
NAT_N_DR = 2 * A_KH - 1
NAT_N_DC = 2 * A_KW - 1


def _natten_kernel(rpb_ref, q_ref, k_ref, v_ref, kc_ref, vc_ref, g_ref, o_ref,
                   tl_ref, tr_ref, vt_ref, kcb_ref, vct_ref, s_scr, p_scr, m_scr, l_scr, *, n_rows):
    h = pl.program_id(0)
    b = pl.program_id(1)
    n_chunks = q_ref.shape[0] // Q_CHUNK

    @pl.when(b == 0)
    def _build():
        ck = lax.broadcasted_iota(jnp.int32, (GRID_W, LANE), 0)
        lane = lax.broadcasted_iota(jnp.int32, (GRID_W, LANE), 1)
        left = lane < GRID_W
        cq = jnp.where(left, lane, lane - GRID_W)
        delta = jnp.clip(ck - cq, -(A_KW - 1), A_KW - 1) + (A_KW - 1)
        start_c = jnp.clip(cq - A_KW // 2, 0, GRID_W - A_KW)
        col_ok = (ck >= start_c) & (ck < start_c + A_KW)
        for d in range(NAT_N_DR):
            t = jnp.zeros((GRID_W, LANE), F32)
            for e in range(NAT_N_DC):
                t = jnp.where(delta == e, rpb_ref[h * (NAT_N_DR * NAT_N_DC) + d * NAT_N_DC + e], t)
            t = jnp.where(col_ok, t * LOG2E, MASK_NEG)
            tl_ref[d] = jnp.where(left, t, 0.0)
            tr_ref[d] = jnp.where(left, 0.0, t)
        tl_ref[NAT_N_DR] = jnp.where(left, MASK_NEG, 0.0)
        tr_ref[NAT_N_DR] = jnp.where(left, 0.0, MASK_NEG)

    for j in range(n_chunks):
        vt_ref[j] = v_ref[j * Q_CHUNK:(j + 1) * Q_CHUNK, :].astype(F32).T.astype(BF16)
    kcb_ref[...] = kc_ref[...].astype(BF16)
    vct_ref[...] = vc_ref[...].T.astype(BF16)
    kh = min(A_KH, n_rows)

    def chunk_of(ci, j):
        return 2 * ci + j

    def rows_of(c):
        if isinstance(c, int):
            return slice(c * Q_CHUNK, (c + 1) * Q_CHUNK)
        return pl.ds(pl.multiple_of(c * Q_CHUNK, Q_CHUNK), Q_CHUNK)

    def window(c):
        return jnp.clip(c - 1, 0, n_chunks - NAT_WIN_CHUNKS)

    def qk(ci, j):
        c = chunk_of(ci, j)
        r0 = c * NAT_ROWS_PER_CHUNK
        w0 = window(c)
        ws = w0 * NAT_ROWS_PER_CHUNK
        ks, biases = [], []
        for w in range(NAT_WIN_CHUNKS):
            ks.append(k_ref[rows_of(w0 + w), :])
            blk_rows = []
            for jk in range(NAT_ROWS_PER_CHUNK):
                rk = ws + w * NAT_ROWS_PER_CHUNK + jk
                blks = []
                for qp in range(NAT_ROWS_PER_CHUNK // 2):
                    idx = []
                    for half in range(2):
                        rq = r0 + 2 * qp + half
                        st = jnp.clip(rq - kh // 2, 0, n_rows - kh)
                        valid = (rk >= st) & (rk < st + kh)
                        idx.append(jnp.where(valid, rk - rq + (A_KH - 1), NAT_N_DR))
                    blks.append(tl_ref[idx[0]] + tr_ref[idx[1]])
                blk_rows.append(jnp.concatenate(blks, axis=1))
            biases.append(jnp.concatenate(blk_rows, axis=0))
        ks.append(kcb_ref[...])
        biases.append(None)
        _stage_qk(q_ref[rows_of(c), :], ks, biases, s_scr.at[j], m_scr.at[j])

    def sm(ci, j):
        _stage_softmax(s_scr.at[j], m_scr.at[j], p_scr.at[j], l_scr.at[j])

    def pv(ci, j):
        c = chunk_of(ci, j)
        w0 = window(c)
        vts = [vt_ref[w0 + w] for w in range(NAT_WIN_CHUNKS)] + [vct_ref[...]]
        o = _stage_pv(vts, p_scr.at[j], l_scr.at[j]).T
        rows = rows_of(c)
        o_ref[rows, :] = (o * g_ref[rows, :]).astype(BF16)

    _pipeline(n_chunks // 2, 2, qk, sm, pv)


def _natten(rpb_flat, zb, cache_k, cache_v, g, layer, nb, t):
    s_ctx = cache_k.shape[2]
    n_rows = t // GRID_W
    return pl.pallas_call(
        functools.partial(_natten_kernel, n_rows=n_rows),
        out_shape=jax.ShapeDtypeStruct((nb * t, GROUP_W), BF16),
        grid=(N_HEADS, nb),
        in_specs=[
            pl.BlockSpec(memory_space=pltpu.SMEM),
            pl.BlockSpec((t, LANE), lambda h, b: (b, h)),
            pl.BlockSpec((t, LANE), lambda h, b: (b, 24 + h)),
            pl.BlockSpec((t, LANE), lambda h, b: (b, 32 + h)),
            pl.BlockSpec((None, None, s_ctx, LANE), lambda h, b: (b, layer, 0, h)),
            pl.BlockSpec((None, None, s_ctx, LANE), lambda h, b: (b, layer, 0, h)),
            pl.BlockSpec((t, LANE), lambda h, b: (b, h)),
        ],
        out_specs=pl.BlockSpec((t, LANE), lambda h, b: (b, h)),
        scratch_shapes=[pltpu.VMEM((NAT_N_DR + 1, GRID_W, LANE), F32),
                        pltpu.VMEM((NAT_N_DR + 1, GRID_W, LANE), F32),
                        pltpu.VMEM((t // Q_CHUNK, LANE, Q_CHUNK), BF16),
                        pltpu.VMEM((s_ctx, LANE), BF16),
                        pltpu.VMEM((LANE, s_ctx), BF16),
                        pltpu.VMEM((2, NAT_WIN_CHUNKS * Q_CHUNK + s_ctx, Q_CHUNK), F32),
                        pltpu.VMEM((2, NAT_WIN_CHUNKS * Q_CHUNK + s_ctx, Q_CHUNK), BF16),
                        pltpu.VMEM((2, 1, Q_CHUNK), F32),
                        pltpu.VMEM((2, 1, Q_CHUNK), F32)],
        compiler_params=_cparams(2, 48),
        name="natten",
    )(rpb_flat, zb, zb, zb, cache_k, cache_v, g)


def _outproj_kernel(ya, yb, yc, yd, w_ref, x_ref, gate_ref, o_ref):
    acc = None
    for n, y in enumerate((ya, yb, yc, yd)):
        part = _dot(y[...], w_ref[n * GROUP_W:(n + 1) * GROUP_W, :].astype(BF16))
        acc = part if acc is None else acc + part
    o_ref[...] = x_ref[...] + gate_ref[...] * acc


def _outproj(ys, w_out, x, mods4, layer, row_fn):
    m = x.shape[0]
    gate_blk0 = 2 * D_MODEL // TN
    y_spec = pl.BlockSpec((TM, GROUP_W), lambda i, j: (i, 0))
    return pl.pallas_call(
        _outproj_kernel,
        out_shape=jax.ShapeDtypeStruct((m, D_MODEL), F32),
        grid=(m // TM, D_MODEL // TN),
        in_specs=[y_spec, y_spec, y_spec, y_spec,
                  pl.BlockSpec((None, D_MODEL, TN), lambda i, j: (layer, 0, j)),
                  pl.BlockSpec((TM, TN), lambda i, j: (i, j)),
                  pl.BlockSpec((None, None, 1, TN),
                               lambda i, j: (layer, row_fn(i * TM), 0, gate_blk0 + j))],
        out_specs=pl.BlockSpec((TM, TN), lambda i, j: (i, j)),
        compiler_params=_cparams(2, 56),
        name="outproj",
    )(*ys, w_out, x, mods4)


def _axial_tables(t, head_dim):
    pos = np.arange(t)
    part = head_dim // 2
    half = part // 2
    freqs = ROPE_THETA ** (-(np.arange(half, dtype=np.float64) * 2.0 / part))
    lane = np.arange(LANE) % head_dim
    p = np.where(lane < part, (pos // GRID_W)[:, None], (pos % GRID_W)[:, None])
    within = lane % part
    ang = p * freqs[within % half][None, :]
    first = (within < half)[None, :]
    cos = np.cos(ang)
    sin = np.sin(ang)
    return (jnp.asarray(cos, F32), jnp.asarray(np.where(first, -sin, 0.0), F32),
            jnp.asarray(np.where(first, 0.0, sin), F32))


def _ctx_layer(x, l, lam_init, w_t, w_out, w_uq, w_ukv, p, mods4, nb, t, prev_states):
    row_fn = lambda r: 0
    h = _norm(x, p["norm_g"], mods4, l, row_fn)
    zf_prev, b_prev = (None, None) if prev_states is None else (prev_states[0], prev_states[1:])
    zq, zf, g = _proj(h, w_t, l, MAIN_TILES, p["gains"], None, False, "main_proj_ctx",
                      seq=t, kv_state=zf_prev)
    qb = _bq_proj(h, w_t, l, p["b_q_norm"], w_uq, None, lat=False)
    kb, vb, ckv_state, kpe_state = _bkv_proj(h, w_t, l, p["b_kv_norm"], w_ukv, None, lat=False,
                                             seq=t, states=b_prev)

    grid = (nb,)
    wide = lambda blk: pl.BlockSpec((t, GROUP_W), lambda b: (b, blk))
    state = lambda w, blk: pl.BlockSpec((None, None, t, w), lambda b: (b, l, 0, blk))
    out_spec = pl.BlockSpec((t, GROUP_W), lambda b: (b, 0))
    m = nb * t
    ya = _attn(zq, wide(0), [(zf, state(GROUP_W, 0), zf, state(GROUP_W, 1))], g, wide(0), m, out_spec, grid,
               nh=N_HEADS, group=1, dk=LANE, t_q=t, name="attn_a_ctx")
    yb = _attn(qb, pl.BlockSpec((t, 2 * GROUP_W), lambda b: (b, 0)),
               [(kb, pl.BlockSpec((t, 2 * GROUP_W), lambda b: (b, 0)), vb, wide(0))],
               g, wide(1), m, out_spec, grid, nh=N_HEADS, group=1, dk=2 * LANE, t_q=t,
               name="attn_b_ctx")
    ckv_w = C_KV_HEADS * HEAD_DIM
    c_k_blk = 4 * GROUP_W // ckv_w
    yc = _attn(zq, wide(1), [(zf, state(ckv_w, c_k_blk), zf, state(ckv_w, c_k_blk + 1))],
               g, wide(2), m, out_spec, grid, nh=N_HEADS, group=N_HEADS // C_KV_HEADS,
               dk=LANE, t_q=t, name="attn_c_ctx")
    yd = _attn(zq, wide(2), [(zf, state(GROUP_W, 2), zf, state(GROUP_W, 3))], g, wide(3), m, out_spec, grid,
               nh=N_HEADS, group=1, dk=LANE, t_q=t,
               diff_params=p["diff"], lam_init=lam_init, name="attn_d_ctx")
    x_new = _outproj((ya, yb, yc, yd), w_out, x, mods4, l, row_fn)
    return x_new, (zf, ckv_state, kpe_state)


def _lat_layer(x, l, lam_init, w_t, w_out, w_uq, w_ukv, p, mods4, caches, rope_tab, tables_d, nb, t):
    row_fn = lambda r: 1 + r // t
    h = _norm(x, p["norm_g"], mods4, l, row_fn)
    zb, g = _proj(h, w_t, l, MAIN_TILES, p["gains"], rope_tab, True, "main_proj_lat")
    qb = _bq_proj(h, w_t, l, p["b_q_norm"], w_uq, tables_d, lat=True)
    kb, vb = _bkv_proj(h, w_t, l, p["b_kv_norm"], w_ukv, tables_d, lat=True)
    kbc, vbc = _cache_kv(caches["b_ckv"], caches["b_kpe"], w_ukv, l)
    s_ctx = caches["a_k"].shape[2]
    m = nb * t

    ya = _natten(p["rpb"], zb, caches["a_k"], caches["a_v"], g, l, nb, t)

    hps = LAT_HEADS_PER_STEP
    grid = (nb, N_HEADS // hps)
    col = lambda blk0, w=hps: pl.BlockSpec((t, w * LANE), lambda b, h: (b, blk0 // w + h))
    cache = lambda w=hps: pl.BlockSpec((None, None, s_ctx, w * LANE), lambda b, h: (b, l, 0, h))
    out_spec = col(0)
    hb = LAT_B_HEADS_PER_STEP
    yb = _attn(qb, col(0, 2 * hb),
               [(kb, col(0, 2 * hb), vb, col(0, hb)),
                (kbc, pl.BlockSpec((s_ctx, 2 * hb * LANE), lambda b, h: (b, h)),
                 vbc, pl.BlockSpec((s_ctx, hb * LANE), lambda b, h: (b, h)))],
               g, col(8, hb), m, col(0, hb), (nb, N_HEADS // hb), nh=hb, group=1, dk=2 * LANE, t_q=t,
               name="attn_b_lat")
    yd = _attn(zb, col(16),
               [(zb, col(40), zb, col(48)),
                (caches["d_k"], cache(), caches["d_v"], cache())],
               g, col(24), m, out_spec, grid, nh=hps, group=1, dk=LANE, t_q=t,
               diff_params=p["diff"], lam_init=lam_init, name="attn_d_lat")
    grp = N_HEADS // C_KV_HEADS
    yc = _attn(zb, col(8, grp),
               [(zb, col(56, 1), zb, col(58, 1)),
                (caches["c_k"], cache(1), caches["c_v"], cache(1))],
               g, col(16, grp), m, col(0, grp), (nb, C_KV_HEADS), nh=grp, group=grp, dk=LANE, t_q=t,
               name="attn_c_lat")
    return _outproj((ya, yb, yc, yd), w_out, x, mods4, l, row_fn)


def kernel(x_prompt, x_sample, cache_a_k, cache_a_v, cache_b_ckv, cache_b_kpe, cache_c_k, cache_c_v, cache_d_k, cache_d_v, c, c_ctx, norm_g, w_ada, b_ada, w_in, w_out, a_rpb, b_q_norm, b_w_uq, b_kv_norm, b_w_ukv, c_q_norm, c_k_norm, d_lq1, d_lk1, d_lq2, d_lk2, d_subln, final_norm_g):
    nb_c, t_c, _ = x_prompt.shape
    nb_l, t_l, _ = x_sample.shape
    s_ctx = cache_a_k.shape[2]

    c8 = jnp.concatenate([c_ctx[None, :], c, jnp.zeros((8 - 1 - nb_l, D_MODEL), F32)], axis=0)
    mods = _mods(c8, w_ada, b_ada)
    mods4 = mods[:, :1 + nb_l].reshape(DEPTH, 1 + nb_l, 1, 3 * D_MODEL)

    w_t = jnp.swapaxes(w_in, 1, 2)
    w_uq = jnp.pad(b_w_uq.reshape(DEPTH, B_Q_LORA, N_HEADS, B_NOPE + B_ROPE),
                   ((0, 0), (0, 0), (0, 0), (0, 2 * LANE - B_NOPE - B_ROPE)))
    w_uq = w_uq.reshape(DEPTH, B_Q_LORA, N_HEADS * 2 * LANE).astype(BF16)
    w_ukv = b_w_ukv.astype(BF16)

    caches = dict(
        a_k=cache_a_k.reshape(nb_l, DEPTH, s_ctx, GROUP_W),
        a_v=cache_a_v.reshape(nb_l, DEPTH, s_ctx, GROUP_W),
        b_ckv=cache_b_ckv,
        b_kpe=jnp.pad(cache_b_kpe, ((0, 0), (0, 0), (0, 0), (0, LANE - B_ROPE))),
        c_k=cache_c_k.reshape(nb_l, DEPTH, s_ctx, C_KV_HEADS * HEAD_DIM),
        c_v=cache_c_v.reshape(nb_l, DEPTH, s_ctx, C_KV_HEADS * HEAD_DIM),
        d_k=cache_d_k.reshape(nb_l, DEPTH, s_ctx, GROUP_W),
        d_v=cache_d_v.reshape(nb_l, DEPTH, s_ctx, GROUP_W),
    )
    tables_c = _axial_tables(t_l, HEAD_DIM)
    tables_d = _axial_tables(t_l, D_QK)
    no_rope = (jnp.ones((t_l, LANE), F32), jnp.zeros((t_l, LANE), F32), jnp.zeros((t_l, LANE), F32))
    rope_tab = jnp.stack(no_rope + tables_c + tables_d)

    xp = x_prompt.reshape(nb_c * t_c, D_MODEL)
    xs = x_sample.reshape(nb_l * t_l, D_MODEL)
    ctx_states = None
    for l in range(DEPTH):
        lam_init = 0.8 - 0.6 * math.exp(-0.3 * l)
        p = dict(
            norm_g=norm_g[l][None, :],
            gains=jnp.stack([jnp.ones((1, LANE), F32), c_q_norm[l][None, :], c_k_norm[l][None, :]]),
            b_q_norm=b_q_norm[l][None, :], b_kv_norm=b_kv_norm[l][None, :],
            rpb=a_rpb[l].reshape(-1),
            diff=(d_lq1[l][None, :], d_lk1[l][None, :], d_lq2[l][None, :], d_lk2[l][None, :],
                  d_subln[l][None, :]),
        )
        xp, ctx_states = _ctx_layer(xp, l, lam_init, w_t, w_out, w_uq[l], w_ukv[l], p, mods4, nb_c, t_c,
                                    ctx_states)
        xs = _lat_layer(xs, l, lam_init, w_t, w_out, w_uq[l], w_ukv[l], p, mods4, caches,
                        rope_tab, tables_d, nb_l, t_l)

    fg = final_norm_g[None, :]
    y_prompt = _norm(xp, fg, out_dtype=F32).reshape(nb_c, t_c, D_MODEL)
    y_sample = _norm(xs, fg, out_dtype=F32).reshape(nb_l, t_l, D_MODEL)

    zf, new_b_ckv, new_b_kpe = ctx_states
    ckv_w = C_KV_HEADS * HEAD_DIM

    def state(col0, width, tail):
        return zf[..., col0:col0 + width].reshape((nb_c, DEPTH, t_c) + tail)

    return (y_prompt, y_sample,
            state(0, GROUP_W, (N_HEADS, HEAD_DIM)), state(GROUP_W, GROUP_W, (N_HEADS, HEAD_DIM)),
            new_b_ckv, new_b_kpe,
            state(4 * GROUP_W, ckv_w, (C_KV_HEADS, HEAD_DIM)),
            state(4 * GROUP_W + ckv_w, ckv_w, (C_KV_HEADS, HEAD_DIM)),
            state(2 * GROUP_W, GROUP_W, (N_HEADS, 2, D_QK)), state(3 * GROUP_W, GROUP_W, (N_HEADS, HEAD_DIM)))
```

```python
import functools
import math

import numpy as np
import jax
import jax.numpy as jnp
from jax import lax
from jax.experimental import pallas as pl
from jax.experimental.pallas import tpu as pltpu

F32 = jnp.float32
BF16 = jnp.bfloat16

D_MODEL = 4096
DEPTH = 2
GRID_W = 64
HEAD_DIM = 128
N_HEADS = 8
GROUP_W = N_HEADS * HEAD_DIM
A_KH = 8
A_KW = 16
B_NOPE = 128
B_ROPE = 64
B_Q_LORA = 768
B_KV_LORA = 512
C_KV_HEADS = 2
D_QK = 64
ROPE_THETA = 10000.0
EPS = 1e-6

LANE = 128
MASK_NEG = -1e30

LOG2E = math.log2(math.e)
QS_A = HEAD_DIM ** -0.5 * LOG2E
QS_B = (B_NOPE + B_ROPE) ** -0.5 * LOG2E
QS_D = D_QK ** -0.5 * LOG2E

TM = 1024
TN = 512
TM_B = 512
Q_CHUNK = 256
LAT_HEADS_PER_STEP = 2
LAT_B_HEADS_PER_STEP = 2
CHUNK_UNROLL = 2
NORM_TM = 512
NORM_ROWS = 16


def _cparams(n_axes, vmem_mib):
    return pltpu.CompilerParams(dimension_semantics=("arbitrary",) * n_axes,
                                vmem_limit_bytes=vmem_mib * 2 ** 20)


def _rms(z, g):
    ms = jnp.mean(z * z, axis=-1, keepdims=True)
    return z * lax.rsqrt(ms + EPS) * g


def _silu(z):
    return z * jax.nn.sigmoid(z)


def _rope(z, cos, sin_lo, sin_hi, half):
    return (z * cos + pltpu.roll(z, LANE - half, 1) * sin_lo
            + pltpu.roll(z, half, 1) * sin_hi)


def _dot(a, b):
    return jnp.dot(a, b, preferred_element_type=F32)


def _dot_nt(a, b):
    return lax.dot_general(a, b, (((1,), (1,)), ((), ())), preferred_element_type=F32)


def _stage_qk(q, ks, biases, s_ref, m_ref):
    off = 0
    m = None
    for k, bias in zip(ks, biases):
        st = _dot_nt(k, q)
        if bias is not None:
            st = st + bias
        s_ref[off:off + k.shape[0], :] = st
        mi = st.max(axis=0, keepdims=True)
        m = mi if m is None else jnp.maximum(m, mi)
        off += k.shape[0]
    m_ref[...] = m


def _stage_softmax(s_ref, m_ref, p_ref, l_ref):
    p = jnp.exp2(s_ref[...] - m_ref[...])
    l_ref[...] = p.sum(axis=0, keepdims=True)
    p_ref[...] = p.astype(BF16)


def _stage_pv(vts, p_ref, l_ref):
    off = 0
    o_t = None
    for vt in vts:
        part = _dot(vt, p_ref[off:off + vt.shape[1], :])
        o_t = part if o_t is None else o_t + part
        off += vt.shape[1]
    return o_t / l_ref[...]


def _pipeline(n_chunks, cpc, qk, sm, pv):
    assert cpc % 2 == 0

    def prev(c, j, back):
        return (c, j - back) if j >= back else (c - 1, j - back + cpc)

    for j in range(cpc):
        qk(0, j)
        if j >= 1:
            sm(0, j - 1)
        if j >= 2:
            pv(0, j - 2)
    if n_chunks > 1:
        def body(c, carry):
            for j in range(cpc):
                qk(c, j)
                sm(*prev(c, j, 1))
                pv(*prev(c, j, 2))
            return carry
        lax.fori_loop(1, n_chunks, body, 0)
    last = n_chunks - 1
    sm(last, cpc - 1)
    pv(last, cpc - 2)
    pv(last, cpc - 1)


def _mods_kernel(c_ref, w_ref, b_ref, o_ref):
    x = _silu(c_ref[...])
    x_hi = x.astype(BF16).astype(F32)
    xs = jnp.concatenate([x_hi, x - x_hi], axis=0).astype(BF16)
    w = w_ref[...]
    w_hi = w.astype(BF16)
    w_lo = (w - w_hi.astype(F32)).astype(BF16)
    r_hi = _dot(xs, w_hi)
    r_lo = _dot(xs, w_lo)
    o_ref[...] = r_hi[:8] + r_hi[8:] + r_lo[:8] + b_ref[...]


def _mods(c8, w_ada, b_ada):
    tn = 512
    n = 3 * D_MODEL
    return pl.pallas_call(
        _mods_kernel,
        out_shape=jax.ShapeDtypeStruct((DEPTH, 8, n), F32),
        grid=(DEPTH, n // tn),
        in_specs=[
            pl.BlockSpec((8, D_MODEL), lambda l, j: (0, 0)),
            pl.BlockSpec((None, D_MODEL, tn), lambda l, j: (l, 0, j)),
            pl.BlockSpec((None, 1, tn), lambda l, j: (l, 0, j)),
        ],
        out_specs=pl.BlockSpec((None, 8, tn), lambda l, j: (l, 0, j)),
        compiler_params=_cparams(2, 48),
        name="mods",
    )(c8, w_ada, b_ada.reshape(DEPTH, 1, n))


def _norm_kernel(*refs, modulate):
    if modulate:
        x_ref, g_ref, mod_ref, o_ref = refs
    else:
        x_ref, g_ref, o_ref = refs
    gain = g_ref[...]
    if modulate:
        gain = gain * (1.0 + mod_ref[:, D_MODEL:2 * D_MODEL])

    def body(r, carry):
        rows = pl.ds(pl.multiple_of(r * NORM_ROWS, NORM_ROWS), NORM_ROWS)
        y = _rms(x_ref[rows, :], gain)
        if modulate:
            y = y + mod_ref[:, 0:D_MODEL]
        o_ref[rows, :] = y.astype(o_ref.dtype)
        return carry

    lax.fori_loop(0, NORM_TM // NORM_ROWS, body, 0, unroll=4)


def _norm(x, g, mods4=None, layer=0, row_fn=None, out_dtype=BF16):
    m = x.shape[0]
    modulate = mods4 is not None
    in_specs = [pl.BlockSpec((NORM_TM, D_MODEL), lambda i: (i, 0)),
                pl.BlockSpec((1, D_MODEL), lambda i: (0, 0))]
    args = [x, g]
    if modulate:
        in_specs.append(pl.BlockSpec((None, None, 1, 3 * D_MODEL),
                                     lambda i: (layer, row_fn(i * NORM_TM), 0, 0)))
        args.append(mods4)
    return pl.pallas_call(
        functools.partial(_norm_kernel, modulate=modulate),
        out_shape=jax.ShapeDtypeStruct((m, D_MODEL), out_dtype),
        grid=(m // NORM_TM,),
        in_specs=in_specs,
        out_specs=pl.BlockSpec((NORM_TM, D_MODEL), lambda i: (i, 0)),
        compiler_params=_cparams(1, 48),
        name="norm_mod" if modulate else "norm_final",
    )(*args)


Q_KINDS = ("aq", "cq", "dq")
KV_KINDS = ("plain", "dk", "ckv")
SCALES = (1.0, QS_A, QS_D)
SLICE_PARAMS = {
    "aq": ((0, 0, 1, 0),) * 4,
    "cq": ((1, 1, 1, 0),) * 4,
    "dq": ((0, 2, 2, 0),) * 4,
    "plain": ((0, 0, 0, 0),) * 4,
    "dk": ((0, 2, 0, 0),) * 4,
    "ckv": ((2, 1, 0, 0),) * 2 + ((0, 0, 0, 0),) * 2,
    "gate": ((0, 0, 0, 1),) * 4,
}
N_SLICES = TN // LANE
ROPE_HALF = (0, 32, 16)


def _proj_kernel(tbl_ref, *refs, lat, n_tiles, n_out, n_in):
    h_ref, w_ref, gains_ref = refs[:3]
    tab_ref = refs[3] if lat else None
    outs = refs[n_in:n_in + n_out]
    acc_refs = refs[-2:]
    s = pl.program_id(0)
    jm = s % n_tiles
    par0 = n_out + 1

    @pl.when(s == 0)
    def _():
        acc_refs[1][...] = jnp.zeros_like(acc_refs[1])

    def step(acc_new, acc_prev):
        acc_new[...] = _dot_nt(h_ref[...], w_ref[...].astype(BF16))
        for k in range(N_SLICES):
            gain_id, rope_id, scale_id, is_gate = (tbl_ref[par0 + 4 * k + f, jm] for f in range(4))
            sl = slice(k * LANE, (k + 1) * LANE)
            z = acc_prev[:, sl]
            r = jnp.where(gain_id != 0, _rms(z, gains_ref[gain_id]), z)
            if lat:
                half = jnp.where(rope_id == 2, ROPE_HALF[2], ROPE_HALF[1])
                r = _rope(r, tab_ref[3 * rope_id], tab_ref[3 * rope_id + 1], tab_ref[3 * rope_id + 2], half)
            scale = jnp.where(scale_id == 1, SCALES[1], jnp.where(scale_id == 2, SCALES[2], SCALES[0]))
            r = jnp.where(is_gate != 0, _silu(z), r * scale)
            for out_ref in outs:
                if len(out_ref.shape) == 2:
                    out_ref[:, sl] = r.astype(out_ref.dtype)
                else:
                    out_ref[:, :, sl] = r.astype(out_ref.dtype).reshape(out_ref.shape[:2] + (LANE,))

    for parity in range(2):
        @pl.when(s % 2 == parity)
        def _(parity=parity):
            step(acc_refs[parity], acc_refs[1 - parity])


ROW_UNIT = 64


def _wt_spec(rows, index_map):
    return pl.BlockSpec((None, pl.Element(rows), pl.Element(D_MODEL)), index_map)


def _proj_table(tiles, groups):
    n_tiles = len(tiles)
    kinds = [kd for kd, _ in tiles]
    rows = []
    for group in groups:
        writes = [t for t, kd in enumerate(kinds) if kd in group]
        pos = {t: n for n, t in enumerate(writes)}
        spare = lambda t: len(writes) + (t > writes[0])
        rows.append([pos.get((jm - 1) % n_tiles, spare((jm - 1) % n_tiles)) for jm in range(n_tiles)])
    assert all(r % ROW_UNIT == 0 for _, r in tiles)
    rows.append([r // ROW_UNIT for _, r in tiles])
    for k in range(N_SLICES):
        for f in range(4):
            rows.append([SLICE_PARAMS[kinds[(jm - 1) % n_tiles]][k][f] for jm in range(n_tiles)])
    return np.asarray(rows, np.int32)


def _proj(h, w_t, layer, tiles, gains, rope_tab, lat, name, seq=None, kv_state=None):
    m = h.shape[0]
    n_i = m // TM
    n_tiles = len(tiles)
    kinds = tuple(kd for kd, _ in tiles)
    groups = [Q_KINDS + KV_KINDS, ("gate",)] if lat else [Q_KINDS, KV_KINDS, ("gate",)]
    n_out = len(groups)
    tbl = _proj_table(tiles, groups)
    counts = [sum(kd in grp for kd in kinds) for grp in groups]

    def i_ep(s):
        return jnp.maximum(s - 1, 0) // n_tiles

    in_specs = [
        pl.BlockSpec((TM, D_MODEL), lambda s, t: (jnp.minimum(s // n_tiles, n_i - 1), 0)),
        _wt_spec(TN, lambda s, t: (layer, t[n_out, s % n_tiles] * ROW_UNIT, 0)),
        pl.BlockSpec((3, 1, LANE), lambda s, t: (0, 0, 0)),
    ]
    args = [h, w_t, gains]
    if lat:
        t_blocks = rope_tab.shape[1] // TM
        in_specs.append(pl.BlockSpec((9, TM, LANE), lambda s, t: (0, i_ep(s) % t_blocks, 0)))
        args.append(rope_tab)
    dtypes = [BF16, F32] if lat else [BF16, F32, F32]
    out_shape = [jax.ShapeDtypeStruct((m, (n + 2) * TN), dt) for n, dt in zip(counts, dtypes)]
    col = lambda s, t, r: jnp.where(s == 0, counts[r], t[r, s % n_tiles])
    out_specs = [pl.BlockSpec((TM, TN), lambda s, t, r=r: (i_ep(s), col(s, t, r))) for r in range(n_out)]
    aliases = {}
    if not lat:
        out_shape[1] = jax.ShapeDtypeStruct((m // seq, DEPTH, seq, (counts[1] + 2) * TN), F32)
        out_specs[1] = pl.BlockSpec((TM // seq, None, seq, TN), lambda s, t: (i_ep(s), layer, 0, col(s, t, 1)))
        if kv_state is not None:
            in_specs.append(pl.BlockSpec(memory_space=pl.ANY))
            args.append(kv_state)
            aliases = {len(args): 1}
    return pl.pallas_call(
        functools.partial(_proj_kernel, lat=lat, n_tiles=n_tiles, n_out=n_out, n_in=len(args)),
        out_shape=tuple(out_shape),
        input_output_aliases=aliases,
        grid_spec=pltpu.PrefetchScalarGridSpec(
            num_scalar_prefetch=1,
            grid=(n_i * n_tiles + 1,),
            in_specs=in_specs,
            out_specs=tuple(out_specs),
            scratch_shapes=[pltpu.VMEM((TM, TN), F32), pltpu.VMEM((TM, TN), F32)]),
        compiler_params=_cparams(1, 60),
        name=name,
    )(jnp.asarray(tbl), *args)


COL = {}
_off = 0
for _name, _size in (("a_q", 1024), ("a_k", 1024), ("a_v", 1024), ("a_z", 1024),
                     ("b_cq", B_Q_LORA), ("b_ckv", B_KV_LORA), ("b_kpe", B_ROPE), ("b_z", 1024),
                     ("c_q", 1024), ("c_k", 256), ("c_v", 256), ("c_z", 1024),
                     ("d_q", 1024), ("d_k", 1024), ("d_v", 1024), ("d_z", 1024)):
    COL[_name] = _off
    _off += _size


def _seg_tiles(kind, name, n_tiles):
    return [(kind, COL[name] + n * TN) for n in range(n_tiles)]


MAIN_TILES = (_seg_tiles("aq", "a_q", 2) + _seg_tiles("cq", "c_q", 2) + _seg_tiles("dq", "d_q", 2)
              + _seg_tiles("plain", "a_k", 2) + _seg_tiles("plain", "a_v", 2)
              + _seg_tiles("dk", "d_k", 2) + _seg_tiles("plain", "d_v", 2) + _seg_tiles("ckv", "c_k", 1)
              + _seg_tiles("gate", "a_z", 2) + _seg_tiles("gate", "b_z", 2)
              + _seg_tiles("gate", "c_z", 2) + _seg_tiles("gate", "d_z", 2))


def _bq_kernel(*refs, lat, n_i):
    if lat:
        h_ref, w_ref, g_ref, wuq_ref, cosd, slod, shid, o_ref = refs[:8]
    else:
        h_ref, w_ref, g_ref, wuq_ref, o_ref = refs[:5]
    wb_ref, z0, z1, acc0, acc1 = refs[-5:]
    zs, accs = (z0, z1), (acc0, acc1)
    s = pl.program_id(0)

    def down(par):
        zs[par][...] = _dot_nt(h_ref[...], wb_ref[...])

    def up(par):
        cq = _rms(zs[par][...], g_ref[...]).astype(BF16)
        accs[par][...] = _dot(cq, wuq_ref[...])

    def epilogue(par):
        acc = accs[par]
        for hh in range(N_HEADS):
            c0 = hh * 2 * LANE
            o_ref[:, c0:c0 + LANE] = (acc[:, c0:c0 + LANE] * QS_B).astype(BF16)
            pe = acc[:, c0 + LANE:c0 + 2 * LANE]
            if lat:
                pe = _rope(pe, cosd[...], slod[...], shid[...], 16)
            o_ref[:, c0 + LANE:c0 + 2 * LANE] = (pe * QS_B).astype(BF16)

    @pl.when(s == 0)
    def _():
        wb_ref[...] = w_ref[...].astype(BF16)
        down(0)

    @pl.when(s == 1)
    def _():
        down(1)
        up(0)

    for par in range(2):
        @pl.when((s >= 2) & (s < n_i) & (s % 2 == par))
        def _(par=par):
            down(par)
            up(1 - par)
            epilogue(par)

    @pl.when(s == n_i)
    def _():
        up((n_i - 1) % 2)
        epilogue(n_i % 2)

    @pl.when(s == n_i + 1)
    def _():
        epilogue((n_i - 1) % 2)


def _bq_proj(h, w_t, layer, gain, w_uq, tables_d, lat):
    m = h.shape[0]
    tm = TM_B
    n_i = m // tm
    assert n_i >= 2
    done = lambda s: jnp.clip(s - 2, 0, n_i - 1)
    in_specs = [
        pl.BlockSpec((tm, D_MODEL), lambda s: (jnp.minimum(s, n_i - 1), 0)),
        _wt_spec(B_Q_LORA, lambda s: (layer, COL["b_cq"], 0)),
        pl.BlockSpec((1, B_Q_LORA), lambda s: (0, 0)),
        pl.BlockSpec((B_Q_LORA, N_HEADS * 2 * LANE), lambda s: (0, 0)),
    ]
    args = [h, w_t, gain, w_uq]
    if lat:
        t_blocks = tables_d[0].shape[0] // tm
        for t in tables_d:
            in_specs.append(pl.BlockSpec((tm, LANE), lambda s: (done(s) % t_blocks, 0)))
            args.append(t)
    return pl.pallas_call(
        functools.partial(_bq_kernel, lat=lat, n_i=n_i),
        out_shape=jax.ShapeDtypeStruct((m, N_HEADS * 2 * LANE), BF16),
        grid=(n_i + 2,),
        in_specs=in_specs,
        out_specs=pl.BlockSpec((tm, N_HEADS * 2 * LANE), lambda s: (done(s), 0)),
        scratch_shapes=[pltpu.VMEM((B_Q_LORA, D_MODEL), BF16),
                        pltpu.VMEM((tm, B_Q_LORA), F32), pltpu.VMEM((tm, B_Q_LORA), F32),
                        pltpu.VMEM((tm, N_HEADS * 2 * LANE), F32),
                        pltpu.VMEM((tm, N_HEADS * 2 * LANE), F32)],
        compiler_params=_cparams(1, 56),
        name="bq_proj_lat" if lat else "bq_proj_ctx",
    )(*args)


def _expand_kv(ckv, kpe, wukv_ref, kb_ref, vb_ref):
    kv = _dot(ckv.astype(BF16), wukv_ref[...])
    kpe_b = kpe.astype(BF16)
    for hh in range(N_HEADS):
        c0 = hh * 2 * LANE
        kb_ref[:, c0:c0 + LANE] = kv[:, c0:c0 + LANE].astype(BF16)
        kb_ref[:, c0 + LANE:c0 + 2 * LANE] = kpe_b
        vb_ref[:, hh * LANE:(hh + 1) * LANE] = kv[:, c0 + LANE:c0 + 2 * LANE].astype(BF16)


def _bkv_kernel(*refs, lat):
    if lat:
        h_ref, w_ref, wpe_ref, g_ref, wukv_ref, cosd, slod, shid, kb_ref, vb_ref, wb_ref, wpeb_ref = refs
    else:
        h_ref, w_ref, wpe_ref, g_ref, wukv_ref = refs[:5]
        kb_ref, vb_ref, ckv_ref, kpe_ref, wb_ref, wpeb_ref = refs[-6:]

    @pl.when(pl.program_id(0) == 0)
    def _():
        wb_ref[...] = w_ref[...].astype(BF16)
        wpeb_ref[...] = wpe_ref[...].astype(BF16)

    h = h_ref[...]
    ckv = _rms(_dot_nt(h, wb_ref[...]), g_ref[...])
    kpe = _dot_nt(h, wpeb_ref[...])
    lane = lax.broadcasted_iota(jnp.int32, (1, LANE), 1)
    kpe = jnp.where(lane < B_ROPE, kpe, 0.0)
    if lat:
        kpe = _rope(kpe, cosd[...], slod[...], shid[...], 16)
    else:
        ckv_ref[...] = ckv.reshape(ckv_ref.shape)
        kpe_ref[...] = kpe[:, :B_ROPE].reshape(kpe_ref.shape)
    _expand_kv(ckv, kpe, wukv_ref, kb_ref, vb_ref)


def _bkv_proj(h, w_t, layer, gain, w_ukv, tables_d, lat, seq=None, states=None):
    m = h.shape[0]
    tm = TM_B
    in_specs = [
        pl.BlockSpec((tm, D_MODEL), lambda i: (i, 0)),
        _wt_spec(B_KV_LORA, lambda i: (layer, COL["b_ckv"], 0)),
        _wt_spec(LANE, lambda i: (layer, COL["b_kpe"], 0)),
        pl.BlockSpec((1, B_KV_LORA), lambda i: (0, 0)),
        pl.BlockSpec((B_KV_LORA, N_HEADS * 2 * LANE), lambda i: (0, 0)),
    ]
    args = [h, w_t, w_t, gain, w_ukv]
    out_shape = [jax.ShapeDtypeStruct((m, N_HEADS * 2 * LANE), BF16),
                 jax.ShapeDtypeStruct((m, GROUP_W), BF16)]
    out_specs = [pl.BlockSpec((tm, N_HEADS * 2 * LANE), lambda i: (i, 0)),
                 pl.BlockSpec((tm, GROUP_W), lambda i: (i, 0))]
    if lat:
        t_blocks = tables_d[0].shape[0] // tm
        for t in tables_d:
            in_specs.append(pl.BlockSpec((tm, LANE), lambda i: (i % t_blocks, 0)))
            args.append(t)
    aliases = {}
    if not lat:
        out_shape += [jax.ShapeDtypeStruct((m // seq, DEPTH, seq, B_KV_LORA), F32),
                      jax.ShapeDtypeStruct((m // seq, DEPTH, seq, B_ROPE), F32)]
        out_specs += [pl.BlockSpec((tm // seq, None, seq, B_KV_LORA), lambda i: (i, layer, 0, 0)),
                      pl.BlockSpec((tm // seq, None, seq, B_ROPE), lambda i: (i, layer, 0, 0))]
        if states is not None:
            for n, st in enumerate(states):
                in_specs.append(pl.BlockSpec(memory_space=pl.ANY))
                aliases[len(args)] = 2 + n
                args.append(st)
    return pl.pallas_call(
        functools.partial(_bkv_kernel, lat=lat),
        out_shape=tuple(out_shape),
        input_output_aliases=aliases,
        grid=(m // tm,),
        in_specs=in_specs,
        out_specs=tuple(out_specs),
        scratch_shapes=[pltpu.VMEM((B_KV_LORA, D_MODEL), BF16), pltpu.VMEM((LANE, D_MODEL), BF16)],
        compiler_params=_cparams(1, 56),
        name="bkv_proj_lat" if lat else "bkv_proj_ctx",
    )(*args)


def _cache_kv_kernel(ckv_ref, kpe_ref, wukv_ref, kb_ref, vb_ref):
    _expand_kv(ckv_ref[...], kpe_ref[...], wukv_ref, kb_ref, vb_ref)


def _cache_kv(cache_ckv, cache_kpe128, w_ukv, layer):
    nb, _, s, _ = cache_ckv.shape
    return pl.pallas_call(
        _cache_kv_kernel,
        out_shape=(jax.ShapeDtypeStruct((nb * s, N_HEADS * 2 * LANE), BF16),
                   jax.ShapeDtypeStruct((nb * s, GROUP_W), BF16)),
        grid=(nb,),
        in_specs=[
            pl.BlockSpec((None, None, s, B_KV_LORA), lambda b: (b, layer, 0, 0)),
            pl.BlockSpec((None, None, s, LANE), lambda b: (b, layer, 0, 0)),
            pl.BlockSpec((B_KV_LORA, N_HEADS * 2 * LANE), lambda b: (0, 0)),
        ],
        out_specs=(pl.BlockSpec((s, N_HEADS * 2 * LANE), lambda b: (b, 0)),
                   pl.BlockSpec((s, GROUP_W), lambda b: (b, 0))),
        compiler_params=_cparams(1, 32),
        name="cache_kv",
    )(cache_ckv, cache_kpe128, w_ukv)


def _attn_kernel(*refs, n_src, nh, group, dk, n_chunks, diff, lam_init, k_staged):
    q_ref = refs[0]
    kv_refs = refs[1:1 + 2 * n_src]
    g_ref = refs[1 + 2 * n_src]
    pos = 2 + 2 * n_src
    if diff:
        lq1, lk1, lq2, lk2, subln = refs[pos:pos + 5]
        pos += 5
        lam = (jnp.exp(jnp.sum(lq1[...] * lk1[...], axis=-1, keepdims=True))
               - jnp.exp(jnp.sum(lq2[...] * lk2[...], axis=-1, keepdims=True)) + lam_init)
        lane = lax.broadcasted_iota(jnp.int32, (1, LANE), 1)
        m_lo = jnp.where(lane < D_QK, 1.0, 0.0).astype(BF16)
        m_hi = jnp.where(lane >= D_QK, 1.0, 0.0).astype(BF16)
    o_ref = refs[pos]
    scratch = list(refs[pos + 1:])
    n_kvh = nh // group

    k_refs, vt_refs = [], []
    for s in range(n_src):
        k_ref, v_ref = kv_refs[2 * s], kv_refs[2 * s + 1]
        if k_staged[s]:
            kb_ref = scratch.pop(0)
            kb_ref[...] = k_ref[...].astype(BF16)
            k_ref = kb_ref
        vt_ref = scratch.pop(0)
        for kh in range(n_kvh):
            v = v_ref[:, kh * LANE:(kh + 1) * LANE].astype(F32)
            vt_ref[kh * LANE:(kh + 1) * LANE, :] = v.T.astype(BF16)
        k_refs.append(k_ref)
        vt_refs.append(vt_ref)

    s_scr, p_scr, m_scr, l_scr = scratch[:4]
    n_w = 2 if diff else 1

    def rows_of(c):
        if isinstance(c, int):
            return slice(c * Q_CHUNK, (c + 1) * Q_CHUNK)
        return pl.ds(pl.multiple_of(c * Q_CHUNK, Q_CHUNK), Q_CHUNK)

    unroll = CHUNK_UNROLL if n_chunks % CHUNK_UNROLL == 0 else 1

    def split(c, j):
        sub, jj = divmod(j, nh * n_w)
        return (c * unroll + sub,) + divmod(jj, n_w)

    def qk(c, j):
        c, hh, w = split(c, j)
        kh = hh // group
        q = q_ref[rows_of(c), hh * dk:(hh + 1) * dk]
        if diff:
            q = q * (m_lo if w == 0 else m_hi)
        ks = [r[:, kh * dk:(kh + 1) * dk] for r in k_refs]
        _stage_qk(q, ks, [None] * n_src, s_scr.at[j % 2], m_scr.at[j % 2])

    def sm(c, j):
        _stage_softmax(s_scr.at[j % 2], m_scr.at[j % 2], p_scr.at[j % 2], l_scr.at[j % 2])

    def pv(c, j):
        c, hh, w = split(c, j)
        kh = hh // group
        vts = [r[kh * LANE:(kh + 1) * LANE, :] for r in vt_refs]
        o_t = _stage_pv(vts, p_scr.at[j % 2], l_scr.at[j % 2])
        if diff:
            o1_scr = scratch[4]
            if w == 0:
                o1_scr[...] = o_t
                return
            o = _rms((o1_scr[...] - lam * o_t).T, subln[...]) * (1.0 - lam_init)
        else:
            o = o_t.T
        rows = rows_of(c)
        osl = slice(hh * LANE, (hh + 1) * LANE)
        o_ref[rows, osl] = (o * g_ref[rows, osl]).astype(BF16)

    _pipeline(n_chunks // unroll, nh * n_w * unroll, qk, sm, pv)


def _attn(q, q_spec, srcs, g, g_spec, out_rows, out_spec, grid, *, nh, group, dk, t_q,
          diff_params=None, lam_init=0.0, name):
    in_specs = [q_spec]
    args = [q]
    scratch_shapes = []
    k_staged = []
    for ka, kspec, va, vspec in srcs:
        in_specs += [kspec, vspec]
        args += [ka, va]
        s_len, k_w = kspec.block_shape[-2:]
        v_w = vspec.block_shape[-1]
        k_staged.append(ka.dtype != BF16)
        if k_staged[-1]:
            scratch_shapes.append(pltpu.VMEM((s_len, k_w), BF16))
        scratch_shapes.append(pltpu.VMEM((v_w, s_len), BF16))
    in_specs.append(g_spec)
    args.append(g)
    s_total = sum(spec.block_shape[-2] for _, spec, _, _ in srcs)
    scratch_shapes += [pltpu.VMEM((2, s_total, Q_CHUNK), F32), pltpu.VMEM((2, s_total, Q_CHUNK), BF16),
                       pltpu.VMEM((2, 1, Q_CHUNK), F32), pltpu.VMEM((2, 1, Q_CHUNK), F32)]
    diff = diff_params is not None
    if diff:
        scratch_shapes.append(pltpu.VMEM((LANE, Q_CHUNK), F32))
        for p in diff_params:
            in_specs.append(pl.BlockSpec(p.shape, lambda *idx: (0, 0)))
            args.append(p)
    return pl.pallas_call(
        functools.partial(_attn_kernel, n_src=len(srcs), nh=nh, group=group, dk=dk,
                          n_chunks=t_q // Q_CHUNK, diff=diff, lam_init=lam_init,
                          k_staged=tuple(k_staged)),
        out_shape=jax.ShapeDtypeStruct((out_rows, GROUP_W), BF16),
        grid=grid,
        in_specs=in_specs,
        out_specs=out_spec,
        scratch_shapes=scratch_shapes,
        compiler_params=_cparams(len(grid), 56),
        name=name,
    )(*args)


NAT_ROWS_PER_CHUNK = Q_CHUNK // GRID_W
NAT_WIN_CHUNKS = 3
NAT_N_DR = 2 * A_KH - 1
NAT_N_DC = 2 * A_KW - 1


def _natten_kernel(rpb_ref, q_ref, k_ref, v_ref, kc_ref, vc_ref, g_ref, o_ref,
                   tl_ref, tr_ref, vt_ref, kcb_ref, vct_ref, s_scr, p_scr, m_scr, l_scr, *, n_rows):
    h = pl.program_id(0)
    b = pl.program_id(1)
    n_chunks = q_ref.shape[0] // Q_CHUNK

    @pl.when(b == 0)
    def _build():
        ck = lax.broadcasted_iota(jnp.int32, (GRID_W, LANE), 0)
        lane = lax.broadcasted_iota(jnp.int32, (GRID_W, LANE), 1)
        left = lane < GRID_W
        cq = jnp.where(left, lane, lane - GRID_W)
        delta = jnp.clip(ck - cq, -(A_KW - 1), A_KW - 1) + (A_KW - 1)
        start_c = jnp.clip(cq - A_KW // 2, 0, GRID_W - A_KW)
        col_ok = (ck >= start_c) & (ck < start_c + A_KW)
        for d in range(NAT_N_DR):
            t = jnp.zeros((GRID_W, LANE), F32)
            for e in range(NAT_N_DC):
                t = jnp.where(delta == e, rpb_ref[h * (NAT_N_DR * NAT_N_DC) + d * NAT_N_DC + e], t)
            t = jnp.where(col_ok, t * LOG2E, MASK_NEG)
            tl_ref[d] = jnp.where(left, t, 0.0)
            tr_ref[d] = jnp.where(left, 0.0, t)
        tl_ref[NAT_N_DR] = jnp.where(left, MASK_NEG, 0.0)
        tr_ref[NAT_N_DR] = jnp.where(left, 0.0, MASK_NEG)

    for j in range(n_chunks):
        vt_ref[j] = v_ref[j * Q_CHUNK:(j + 1) * Q_CHUNK, :].astype(F32).T.astype(BF16)
    kcb_ref[...] = kc_ref[...].astype(BF16)
    vct_ref[...] = vc_ref[...].T.astype(BF16)
    kh = min(A_KH, n_rows)

    def chunk_of(ci, j):
        return 2 * ci + j

    def rows_of(c):
        if isinstance(c, int):
            return slice(c * Q_CHUNK, (c + 1) * Q_CHUNK)
        return pl.ds(pl.multiple_of(c * Q_CHUNK, Q_CHUNK), Q_CHUNK)

    def window(c):
        return jnp.clip(c - 1, 0, n_chunks - NAT_WIN_CHUNKS)

    def qk(ci, j):
        c = chunk_of(ci, j)
        r0 = c * NAT_ROWS_PER_CHUNK
        w0 = window(c)
        ws = w0 * NAT_ROWS_PER_CHUNK
        ks, biases = [], []
        for w in range(NAT_WIN_CHUNKS):
            ks.append(k_ref[rows_of(w0 + w), :])
            blk_rows = []
            for jk in range(NAT_ROWS_PER_CHUNK):
                rk = ws + w * NAT_ROWS_PER_CHUNK + jk
                blks = []
                for qp in range(NAT_ROWS_PER_CHUNK // 2):
                    idx = []
                    for half in range(2):
                        rq = r0 + 2 * qp + half
                        st = jnp.clip(rq - kh // 2, 0, n_rows - kh)
                        valid = (rk >= st) & (rk < st + kh)
                        idx.append(jnp.where(valid, rk - rq + (A_KH - 1), NAT_N_DR))
                    blks.append(tl_ref[idx[0]] + tr_ref[idx[1]])
                blk_rows.append(jnp.concatenate(blks, axis=1))
            biases.append(jnp.concatenate(blk_rows, axis=0))
        ks.append(kcb_ref[...])
        biases.append(None)
        _stage_qk(q_ref[rows_of(c), :], ks, biases, s_scr.at[j], m_scr.at[j])

    def sm(ci, j):
        _stage_softmax(s_scr.at[j], m_scr.at[j], p_scr.at[j], l_scr.at[j])

    def pv(ci, j):
        c = chunk_of(ci, j)
        w0 = window(c)
        vts = [vt_ref[w0 + w] for w in range(NAT_WIN_CHUNKS)] + [vct_ref[...]]
        o = _stage_pv(vts, p_scr.at[j], l_scr.at[j]).T
        rows = rows_of(c)
        o_ref[rows, :] = (o * g_ref[rows, :]).astype(BF16)

    _pipeline(n_chunks // 2, 2, qk, sm, pv)


def _natten(rpb_flat, zb, cache_k, cache_v, g, layer, nb, t):
    s_ctx = cache_k.shape[2]
    n_rows = t // GRID_W
    return pl.pallas_call(
        functools.partial(_natten_kernel, n_rows=n_rows),
        out_shape=jax.ShapeDtypeStruct((nb * t, GROUP_W), BF16),
        grid=(N_HEADS, nb),
        in_specs=[
            pl.BlockSpec(memory_space=pltpu.SMEM),
            pl.BlockSpec((t, LANE), lambda h, b: (b, h)),
            pl.BlockSpec((t, LANE), lambda h, b: (b, 24 + h)),
            pl.BlockSpec((t, LANE), lambda h, b: (b, 32 + h)),
            pl.BlockSpec((None, None, s_ctx, LANE), lambda h, b: (b, layer, 0, h)),
            pl.BlockSpec((None, None, s_ctx, LANE), lambda h, b: (b, layer, 0, h)),
            pl.BlockSpec((t, LANE), lambda h, b: (b, h)),
        ],
        out_specs=pl.BlockSpec((t, LANE), lambda h, b: (b, h)),
        scratch_shapes=[pltpu.VMEM((NAT_N_DR + 1, GRID_W, LANE), F32),
                        pltpu.VMEM((NAT_N_DR + 1, GRID_W, LANE), F32),
                        pltpu.VMEM((t // Q_CHUNK, LANE, Q_CHUNK), BF16),
                        pltpu.VMEM((s_ctx, LANE), BF16),
                        pltpu.VMEM((LANE, s_ctx), BF16),
                        pltpu.VMEM((2, NAT_WIN_CHUNKS * Q_CHUNK + s_ctx, Q_CHUNK), F32),
                        pltpu.VMEM((2, NAT_WIN_CHUNKS * Q_CHUNK + s_ctx, Q_CHUNK), BF16),
                        pltpu.VMEM((2, 1, Q_CHUNK), F32),
                        pltpu.VMEM((2, 1, Q_CHUNK), F32)],
        compiler_params=_cparams(2, 48),
        name="natten",
    )(rpb_flat, zb, zb, zb, cache_k, cache_v, g)


def _outproj_kernel(ya, yb, yc, yd, w_ref, x_ref, gate_ref, o_ref):
    acc = None
    for n, y in enumerate((ya, yb, yc, yd)):
        part = _dot(y[...], w_ref[n * GROUP_W:(n + 1) * GROUP_W, :].astype(BF16))
        acc = part if acc is None else acc + part
    o_ref[...] = x_ref[...] + gate_ref[...] * acc


def _outproj(ys, w_out, x, mods4, layer, row_fn):
    m = x.shape[0]
    gate_blk0 = 2 * D_MODEL // TN
    y_spec = pl.BlockSpec((TM, GROUP_W), lambda i, j: (i, 0))
    return pl.pallas_call(
        _outproj_kernel,
        out_shape=jax.ShapeDtypeStruct((m, D_MODEL), F32),
        grid=(m // TM, D_MODEL // TN),
        in_specs=[y_spec, y_spec, y_spec, y_spec,
                  pl.BlockSpec((None, D_MODEL, TN), lambda i, j: (layer, 0, j)),
                  pl.BlockSpec((TM, TN), lambda i, j: (i, j)),
                  pl.BlockSpec((None, None, 1, TN),
                               lambda i, j: (layer, row_fn(i * TM), 0, gate_blk0 + j))],
        out_specs=pl.BlockSpec((TM, TN), lambda i, j: (i, j)),
        compiler_params=_cparams(2, 56),
        name="outproj",
    )(*ys, w_out, x, mods4)


def _axial_tables(t, head_dim):
    pos = np.arange(t)
    part = head_dim // 2
    half = part // 2
    freqs = ROPE_THETA ** (-(np.arange(half, dtype=np.float64) * 2.0 / part))
    lane = np.arange(LANE) % head_dim
    p = np.where(lane < part, (pos // GRID_W)[:, None], (pos % GRID_W)[:, None])
    within = lane % part
    ang = p * freqs[within % half][None, :]
    first = (within < half)[None, :]
    cos = np.cos(ang)
    sin = np.sin(ang)
    return (jnp.asarray(cos, F32), jnp.asarray(np.where(first, -sin, 0.0), F32),
            jnp.asarray(np.where(first, 0.0, sin), F32))


def _ctx_layer(x, l, lam_init, w_t, w_out, w_uq, w_ukv, p, mods4, nb, t, prev_states):
    row_fn = lambda r: 0
    h = _norm(x, p["norm_g"], mods4, l, row_fn)
    zf_prev, b_prev = (None, None) if prev_states is None else (prev_states[0], prev_states[1:])
    zq, zf, g = _proj(h, w_t, l, MAIN_TILES, p["gains"], None, False, "main_proj_ctx",
                      seq=t, kv_state=zf_prev)
    qb = _bq_proj(h, w_t, l, p["b_q_norm"], w_uq, None, lat=False)
    kb, vb, ckv_state, kpe_state = _bkv_proj(h, w_t, l, p["b_kv_norm"], w_ukv, None, lat=False,
                                             seq=t, states=b_prev)

    grid = (nb,)
    wide = lambda blk: pl.BlockSpec((t, GROUP_W), lambda b: (b, blk))
    state = lambda w, blk: pl.BlockSpec((None, None, t, w), lambda b: (b, l, 0, blk))
    out_spec = pl.BlockSpec((t, GROUP_W), lambda b: (b, 0))
    m = nb * t
    ya = _attn(zq, wide(0), [(zf, state(GROUP_W, 0), zf, state(GROUP_W, 1))], g, wide(0), m, out_spec, grid,
               nh=N_HEADS, group=1, dk=LANE, t_q=t, name="attn_a_ctx")
    yb = _attn(qb, pl.BlockSpec((t, 2 * GROUP_W), lambda b: (b, 0)),
               [(kb, pl.BlockSpec((t, 2 * GROUP_W), lambda b: (b, 0)), vb, wide(0))],
               g, wide(1), m, out_spec, grid, nh=N_HEADS, group=1, dk=2 * LANE, t_q=t,
               name="attn_b_ctx")
    ckv_w = C_KV_HEADS * HEAD_DIM
    c_k_blk = 4 * GROUP_W // ckv_w
    yc = _attn(zq, wide(1), [(zf, state(ckv_w, c_k_blk), zf, state(ckv_w, c_k_blk + 1))],
               g, wide(2), m, out_spec, grid, nh=N_HEADS, group=N_HEADS // C_KV_HEADS,
               dk=LANE, t_q=t, name="attn_c_ctx")
    yd = _attn(zq, wide(2), [(zf, state(GROUP_W, 2), zf, state(GROUP_W, 3))], g, wide(3), m, out_spec, grid,
               nh=N_HEADS, group=1, dk=LANE, t_q=t,
               diff_params=p["diff"], lam_init=lam_init, name="attn_d_ctx")
    x_new = _outproj((ya, yb, yc, yd), w_out, x, mods4, l, row_fn)
    return x_new, (zf, ckv_state, kpe_state)


def _lat_layer(x, l, lam_init, w_t, w_out, w_uq, w_ukv, p, mods4, caches, rope_tab, tables_d, nb, t):
    row_fn = lambda r: 1 + r // t
    h = _norm(x, p["norm_g"], mods4, l, row_fn)
    zb, g = _proj(h, w_t, l, MAIN_TILES, p["gains"], rope_tab, True, "main_proj_lat")
    qb = _bq_proj(h, w_t, l, p["b_q_norm"], w_uq, tables_d, lat=True)
    kb, vb = _bkv_proj(h, w_t, l, p["b_kv_norm"], w_ukv, tables_d, lat=True)
    kbc, vbc = _cache_kv(caches["b_ckv"], caches["b_kpe"], w_ukv, l)
    s_ctx = caches["a_k"].shape[2]
    m = nb * t

    ya = _natten(p["rpb"], zb, caches["a_k"], caches["a_v"], g, l, nb, t)

    hps = LAT_HEADS_PER_STEP
    grid = (nb, N_HEADS // hps)
    col = lambda blk0, w=hps: pl.BlockSpec((t, w * LANE), lambda b, h: (b, blk0 // w + h))
    cache = lambda w=hps: pl.BlockSpec((None, None, s_ctx, w * LANE), lambda b, h: (b, l, 0, h))
    out_spec = col(0)
    hb = LAT_B_HEADS_PER_STEP
    yb = _attn(qb, col(0, 2 * hb),
               [(kb, col(0, 2 * hb), vb, col(0, hb)),
                (kbc, pl.BlockSpec((s_ctx, 2 * hb * LANE), lambda b, h: (b, h)),
                 vbc, pl.BlockSpec((s_ctx, hb * LANE), lambda b, h: (b, h)))],
               g, col(8, hb), m, col(0, hb), (nb, N_HEADS // hb), nh=hb, group=1, dk=2 * LANE, t_q=t,
               name="attn_b_lat")
    yd = _attn(zb, col(16),
               [(zb, col(40), zb, col(48)),
                (caches["d_k"], cache(), caches["d_v"], cache())],
               g, col(24), m, out_spec, grid, nh=hps, group=1, dk=LANE, t_q=t,
               diff_params=p["diff"], lam_init=lam_init, name="attn_d_lat")
    grp = N_HEADS // C_KV_HEADS
    yc = _attn(zb, col(8, grp),
               [(zb, col(56, 1), zb, col(58, 1)),
                (caches["c_k"], cache(1), caches["c_v"], cache(1))],
               g, col(16, grp), m, col(0, grp), (nb, C_KV_HEADS), nh=grp, group=grp, dk=LANE, t_q=t,
               name="attn_c_lat")
    return _outproj((ya, yb, yc, yd), w_out, x, mods4, l, row_fn)


def kernel(x_prompt, x_sample, cache_a_k, cache_a_v, cache_b_ckv, cache_b_kpe, cache_c_k, cache_c_v, cache_d_k, cache_d_v, c, c_ctx, norm_g, w_ada, b_ada, w_in, w_out, a_rpb, b_q_norm, b_w_uq, b_kv_norm, b_w_ukv, c_q_norm, c_k_norm, d_lq1, d_lk1, d_lq2, d_lk2, d_subln, final_norm_g):
    nb_c, t_c, _ = x_prompt.shape
    nb_l, t_l, _ = x_sample.shape
    s_ctx = cache_a_k.shape[2]

    c8 = jnp.concatenate([c_ctx[None, :], c, jnp.zeros((8 - 1 - nb_l, D_MODEL), F32)], axis=0)
    mods = _mods(c8, w_ada, b_ada)
    mods4 = mods[:, :1 + nb_l].reshape(DEPTH, 1 + nb_l, 1, 3 * D_MODEL)

    w_t = jnp.swapaxes(w_in, 1, 2)
    w_uq = jnp.pad(b_w_uq.reshape(DEPTH, B_Q_LORA, N_HEADS, B_NOPE + B_ROPE),
                   ((0, 0), (0, 0), (0, 0), (0, 2 * LANE - B_NOPE - B_ROPE)))
    w_uq = w_uq.reshape(DEPTH, B_Q_LORA, N_HEADS * 2 * LANE).astype(BF16)
    w_ukv = b_w_ukv.astype(BF16)

    caches = dict(
        a_k=cache_a_k.reshape(nb_l, DEPTH, s_ctx, GROUP_W),
        a_v=cache_a_v.reshape(nb_l, DEPTH, s_ctx, GROUP_W),
        b_ckv=cache_b_ckv,
        b_kpe=jnp.pad(cache_b_kpe, ((0, 0), (0, 0), (0, 0), (0, LANE - B_ROPE))),
        c_k=cache_c_k.reshape(nb_l, DEPTH, s_ctx, C_KV_HEADS * HEAD_DIM),
        c_v=cache_c_v.reshape(nb_l, DEPTH, s_ctx, C_KV_HEADS * HEAD_DIM),
        d_k=cache_d_k.reshape(nb_l, DEPTH, s_ctx, GROUP_W),
        d_v=cache_d_v.reshape(nb_l, DEPTH, s_ctx, GROUP_W),
    )
    tables_c = _axial_tables(t_l, HEAD_DIM)
    tables_d = _axial_tables(t_l, D_QK)
    no_rope = (jnp.ones((t_l, LANE), F32), jnp.zeros((t_l, LANE), F32), jnp.zeros((t_l, LANE), F32))
    rope_tab = jnp.stack(no_rope + tables_c + tables_d)

    xp = x_prompt.reshape(nb_c * t_c, D_MODEL)
    xs = x_sample.reshape(nb_l * t_l, D_MODEL)
    ctx_states = None
    for l in range(DEPTH):
        lam_init = 0.8 - 0.6 * math.exp(-0.3 * l)
        p = dict(
            norm_g=norm_g[l][None, :],
            gains=jnp.stack([jnp.ones((1, LANE), F32), c_q_norm[l][None, :], c_k_norm[l][None, :]]),
            b_q_norm=b_q_norm[l][None, :], b_kv_norm=b_kv_norm[l][None, :],
            rpb=a_rpb[l].reshape(-1),
            diff=(d_lq1[l][None, :], d_lk1[l][None, :], d_lq2[l][None, :], d_lk2[l][None, :],
                  d_subln[l][None, :]),
        )
        xp, ctx_states = _ctx_layer(xp, l, lam_init, w_t, w_out, w_uq[l], w_ukv[l], p, mods4, nb_c, t_c,
                                    ctx_states)
        xs = _lat_layer(xs, l, lam_init, w_t, w_out, w_uq[l], w_ukv[l], p, mods4, caches,
                        rope_tab, tables_d, nb_l, t_l)

    fg = final_norm_g[None, :]
    y_prompt = _norm(xp, fg, out_dtype=F32).reshape(nb_c, t_c, D_MODEL)
    y_sample = _norm(xs, fg, out_dtype=F32).reshape(nb_l, t_l, D_MODEL)

    zf, new_b_ckv, new_b_kpe = ctx_states
    ckv_w = C_KV_HEADS * HEAD_DIM

    def state(col0, width, tail):
        return zf[..., col0:col0 + width].reshape((nb_c, DEPTH, t_c) + tail)

    return (y_prompt, y_sample,
            state(0, GROUP_W, (N_HEADS, HEAD_DIM)), state(GROUP_W, GROUP_W, (N_HEADS, HEAD_DIM)),
            new_b_ckv, new_b_kpe,
            state(4 * GROUP_W, ckv_w, (C_KV_HEADS, HEAD_DIM)),
            state(4 * GROUP_W + ckv_w, ckv_w, (C_KV_HEADS, HEAD_DIM)),
            state(2 * GROUP_W, GROUP_W, (N_HEADS, 2, D_QK)), state(3 * GROUP_W, GROUP_W, (N_HEADS, HEAD_DIM)))
```

```python
import functools
import math

import numpy as np
import jax
import jax.numpy as jnp
from jax import lax
from jax.experimental import pallas as pl
from jax.experimental.pallas import tpu as pltpu

F32 = jnp.float32
BF16 = jnp.bfloat16

D_MODEL = 4096
DEPTH = 2
GRID_W = 64
HEAD_DIM = 128
N_HEADS = 8
GROUP_W = N_HEADS * HEAD_DIM
A_KH = 8
A_KW = 16
B_NOPE = 128
B_ROPE = 64
B_Q_LORA = 768
B_KV_LORA = 512
C_KV_HEADS = 2
D_QK = 64
ROPE_THETA = 10000.0
EPS = 1e-6

LANE = 128
MASK_NEG = -1e30

LOG2E = math.log2(math.e)
QS_A = HEAD_DIM ** -0.5 * LOG2E
QS_B = (B_NOPE + B_ROPE) ** -0.5 * LOG2E
QS_D = D_QK ** -0.5 * LOG2E

TM = 1024
TN = 512
TM_B = 512
Q_CHUNK = 256
LAT_HEADS_PER_STEP = 2
LAT_B_HEADS_PER_STEP = 4
CHUNK_UNROLL = 2
NORM_TM = 512
NORM_ROWS = 16


def _cparams(n_axes, vmem_mib):
    return pltpu.CompilerParams(dimension_semantics=("arbitrary",) * n_axes,
                                vmem_limit_bytes=vmem_mib * 2 ** 20)


def _rms(z, g):
    ms = jnp.mean(z * z, axis=-1, keepdims=True)
    return z * lax.rsqrt(ms + EPS) * g


def _silu(z):
    return z * jax.nn.sigmoid(z)


def _rope(z, cos, sin_lo, sin_hi, half):
    return (z * cos + pltpu.roll(z, LANE - half, 1) * sin_lo
            + pltpu.roll(z, half, 1) * sin_hi)


def _dot(a, b):
    return jnp.dot(a, b, preferred_element_type=F32)


def _dot_nt(a, b):
    return lax.dot_general(a, b, (((1,), (1,)), ((), ())), preferred_element_type=F32)


def _stage_qk(q, ks, biases, s_ref, m_ref):
    off = 0
    m = None
    for k, bias in zip(ks, biases):
        st = _dot_nt(k, q)
        if bias is not None:
            st = st + bias
        s_ref[off:off + k.shape[0], :] = st
        mi = st.max(axis=0, keepdims=True)
        m = mi if m is None else jnp.maximum(m, mi)
        off += k.shape[0]
    m_ref[...] = m


def _stage_softmax(s_ref, m_ref, p_ref, l_ref):
    p = jnp.exp2(s_ref[...] - m_ref[...])
    l_ref[...] = p.sum(axis=0, keepdims=True)
    p_ref[...] = p.astype(BF16)


def _stage_pv(vts, p_ref, l_ref):
    off = 0
    o_t = None
    for vt in vts:
        part = _dot(vt, p_ref[off:off + vt.shape[1], :])
        o_t = part if o_t is None else o_t + part
        off += vt.shape[1]
    return o_t / l_ref[...]


def _pipeline(n_chunks, cpc, qk, sm, pv):
    assert cpc % 2 == 0

    def prev(c, j, back):
        return (c, j - back) if j >= back else (c - 1, j - back + cpc)

    for j in range(cpc):
        qk(0, j)
        if j >= 1:
            sm(0, j - 1)
        if j >= 2:
            pv(0, j - 2)
    if n_chunks > 1:
        def body(c, carry):
            for j in range(cpc):
                qk(c, j)
                sm(*prev(c, j, 1))
                pv(*prev(c, j, 2))
            return carry
        lax.fori_loop(1, n_chunks, body, 0)
    last = n_chunks - 1
    sm(last, cpc - 1)
    pv(last, cpc - 2)
    pv(last, cpc - 1)


def _mods_kernel(c_ref, w_ref, b_ref, o_ref):
    x = _silu(c_ref[...])
    x_hi = x.astype(BF16).astype(F32)
    xs = jnp.concatenate([x_hi, x - x_hi], axis=0).astype(BF16)
    w = w_ref[...]
    w_hi = w.astype(BF16)
    w_lo = (w - w_hi.astype(F32)).astype(BF16)
    r_hi = _dot(xs, w_hi)
    r_lo = _dot(xs, w_lo)
    o_ref[...] = r_hi[:8] + r_hi[8:] + r_lo[:8] + b_ref[...]


def _mods(c8, w_ada, b_ada):
    tn = 512
    n = 3 * D_MODEL
    return pl.pallas_call(
        _mods_kernel,
        out_shape=jax.ShapeDtypeStruct((DEPTH, 8, n), F32),
        grid=(DEPTH, n // tn),
        in_specs=[
            pl.BlockSpec((8, D_MODEL), lambda l, j: (0, 0)),
            pl.BlockSpec((None, D_MODEL, tn), lambda l, j: (l, 0, j)),
            pl.BlockSpec((None, 1, tn), lambda l, j: (l, 0, j)),
        ],
        out_specs=pl.BlockSpec((None, 8, tn), lambda l, j: (l, 0, j)),
        compiler_params=_cparams(2, 48),
        name="mods",
    )(c8, w_ada, b_ada.reshape(DEPTH, 1, n))


def _norm_kernel(*refs, modulate):
    if modulate:
        x_ref, g_ref, mod_ref, o_ref = refs
    else:
        x_ref, g_ref, o_ref = refs
    gain = g_ref[...]
    if modulate:
        gain = gain * (1.0 + mod_ref[:, D_MODEL:2 * D_MODEL])

    def body(r, carry):
        rows = pl.ds(pl.multiple_of(r * NORM_ROWS, NORM_ROWS), NORM_ROWS)
        y = _rms(x_ref[rows, :], gain)
        if modulate:
            y = y + mod_ref[:, 0:D_MODEL]
        o_ref[rows, :] = y.astype(o_ref.dtype)
        return carry

    lax.fori_loop(0, NORM_TM // NORM_ROWS, body, 0, unroll=4)


def _norm(x, g, mods4=None, layer=0, row_fn=None, out_dtype=BF16):
    m = x.shape[0]
    modulate = mods4 is not None
    in_specs = [pl.BlockSpec((NORM_TM, D_MODEL), lambda i: (i, 0)),
                pl.BlockSpec((1, D_MODEL), lambda i: (0, 0))]
    args = [x, g]
    if modulate:
        in_specs.append(pl.BlockSpec((None, None, 1, 3 * D_MODEL),
                                     lambda i: (layer, row_fn(i * NORM_TM), 0, 0)))
        args.append(mods4)
    return pl.pallas_call(
        functools.partial(_norm_kernel, modulate=modulate),
        out_shape=jax.ShapeDtypeStruct((m, D_MODEL), out_dtype),
        grid=(m // NORM_TM,),
        in_specs=in_specs,
        out_specs=pl.BlockSpec((NORM_TM, D_MODEL), lambda i: (i, 0)),
        compiler_params=_cparams(1, 48),
        name="norm_mod" if modulate else "norm_final",
    )(*args)


Q_KINDS = ("aq", "cq", "dq")
KV_KINDS = ("plain", "dk", "ckv")
SCALES = (1.0, QS_A, QS_D)
SLICE_PARAMS = {
    "aq": ((0, 0, 1, 0),) * 4,
    "cq": ((1, 1, 1, 0),) * 4,
    "dq": ((0, 2, 2, 0),) * 4,
    "plain": ((0, 0, 0, 0),) * 4,
    "dk": ((0, 2, 0, 0),) * 4,
    "ckv": ((2, 1, 0, 0),) * 2 + ((0, 0, 0, 0),) * 2,
    "gate": ((0, 0, 0, 1),) * 4,
}
N_SLICES = TN // LANE
ROPE_HALF = (0, 32, 16)


def _proj_kernel(tbl_ref, *refs, lat, n_tiles, n_out, n_in):
    h_ref, w_ref, gains_ref = refs[:3]
    tab_ref = refs[3] if lat else None
    outs = refs[n_in:n_in + n_out]
    acc_refs = refs[-2:]
    s = pl.program_id(0)
    jm = s % n_tiles
    par0 = n_out + 1

    @pl.when(s == 0)
    def _():
        acc_refs[1][...] = jnp.zeros_like(acc_refs[1])

    def step(acc_new, acc_prev):
        acc_new[...] = _dot_nt(h_ref[...], w_ref[...].astype(BF16))
        for k in range(N_SLICES):
            gain_id, rope_id, scale_id, is_gate = (tbl_ref[par0 + 4 * k + f, jm] for f in range(4))
            sl = slice(k * LANE, (k + 1) * LANE)
            z = acc_prev[:, sl]
            r = jnp.where(gain_id != 0, _rms(z, gains_ref[gain_id]), z)
            if lat:
                half = jnp.where(rope_id == 2, ROPE_HALF[2], ROPE_HALF[1])
                r = _rope(r, tab_ref[3 * rope_id], tab_ref[3 * rope_id + 1], tab_ref[3 * rope_id + 2], half)
            scale = jnp.where(scale_id == 1, SCALES[1], jnp.where(scale_id == 2, SCALES[2], SCALES[0]))
            r = jnp.where(is_gate != 0, _silu(z), r * scale)
            for out_ref in outs:
                if len(out_ref.shape) == 2:
                    out_ref[:, sl] = r.astype(out_ref.dtype)
                else:
                    out_ref[:, :, sl] = r.astype(out_ref.dtype).reshape(out_ref.shape[:2] + (LANE,))

    for parity in range(2):
        @pl.when(s % 2 == parity)
        def _(parity=parity):
            step(acc_refs[parity], acc_refs[1 - parity])


ROW_UNIT = 64


def _wt_spec(rows, index_map):
    return pl.BlockSpec((None, pl.Element(rows), pl.Element(D_MODEL)), index_map)


def _proj_table(tiles, groups):
    n_tiles = len(tiles)
    kinds = [kd for kd, _ in tiles]
    rows = []
    for group in groups:
        writes = [t for t, kd in enumerate(kinds) if kd in group]
        pos = {t: n for n, t in enumerate(writes)}
        spare = lambda t: len(writes) + (t > writes[0])
        rows.append([pos.get((jm - 1) % n_tiles, spare((jm - 1) % n_tiles)) for jm in range(n_tiles)])
    assert all(r % ROW_UNIT == 0 for _, r in tiles)
    rows.append([r // ROW_UNIT for _, r in tiles])
    for k in range(N_SLICES):
        for f in range(4):
            rows.append([SLICE_PARAMS[kinds[(jm - 1) % n_tiles]][k][f] for jm in range(n_tiles)])
    return np.asarray(rows, np.int32)


def _proj(h, w_t, layer, tiles, gains, rope_tab, lat, name, seq=None, kv_state=None):
    m = h.shape[0]
    n_i = m // TM
    n_tiles = len(tiles)
    kinds = tuple(kd for kd, _ in tiles)
    groups = [Q_KINDS + KV_KINDS, ("gate",)] if lat else [Q_KINDS, KV_KINDS, ("gate",)]
    n_out = len(groups)
    tbl = _proj_table(tiles, groups)
    counts = [sum(kd in grp for kd in kinds) for grp in groups]

    def i_ep(s):
        return jnp.maximum(s - 1, 0) // n_tiles

    in_specs = [
        pl.BlockSpec((TM, D_MODEL), lambda s, t: (jnp.minimum(s // n_tiles, n_i - 1), 0)),
        _wt_spec(TN, lambda s, t: (layer, t[n_out, s % n_tiles] * ROW_UNIT, 0)),
        pl.BlockSpec((3, 1, LANE), lambda s, t: (0, 0, 0)),
    ]
    args = [h, w_t, gains]
    if lat:
        t_blocks = rope_tab.shape[1] // TM
        in_specs.append(pl.BlockSpec((9, TM, LANE), lambda s, t: (0, i_ep(s) % t_blocks, 0)))
        args.append(rope_tab)
    dtypes = [BF16, F32] if lat else [BF16, F32, F32]
    out_shape = [jax.ShapeDtypeStruct((m, (n + 2) * TN), dt) for n, dt in zip(counts, dtypes)]
    col = lambda s, t, r: jnp.where(s == 0, counts[r], t[r, s % n_tiles])
    out_specs = [pl.BlockSpec((TM, TN), lambda s, t, r=r: (i_ep(s), col(s, t, r))) for r in range(n_out)]
    aliases = {}
    if not lat:
        out_shape[1] = jax.ShapeDtypeStruct((m // seq, DEPTH, seq, (counts[1] + 2) * TN), F32)
        out_specs[1] = pl.BlockSpec((TM // seq, None, seq, TN), lambda s, t: (i_ep(s), layer, 0, col(s, t, 1)))
        if kv_state is not None:
            in_specs.append(pl.BlockSpec(memory_space=pl.ANY))
            args.append(kv_state)
            aliases = {len(args): 1}
    return pl.pallas_call(
        functools.partial(_proj_kernel, lat=lat, n_tiles=n_tiles, n_out=n_out, n_in=len(args)),
        out_shape=tuple(out_shape),
        input_output_aliases=aliases,
        grid_spec=pltpu.PrefetchScalarGridSpec(
            num_scalar_prefetch=1,
            grid=(n_i * n_tiles + 1,),
            in_specs=in_specs,
            out_specs=tuple(out_specs),
            scratch_shapes=[pltpu.VMEM((TM, TN), F32), pltpu.VMEM((TM, TN), F32)]),
        compiler_params=_cparams(1, 60),
        name=name,
    )(jnp.asarray(tbl), *args)


COL = {}
_off = 0
for _name, _size in (("a_q", 1024), ("a_k", 1024), ("a_v", 1024), ("a_z", 1024),
                     ("b_cq", B_Q_LORA), ("b_ckv", B_KV_LORA), ("b_kpe", B_ROPE), ("b_z", 1024),
                     ("c_q", 1024), ("c_k", 256), ("c_v", 256), ("c_z", 1024),
                     ("d_q", 1024), ("d_k", 1024), ("d_v", 1024), ("d_z", 1024)):
    COL[_name] = _off
    _off += _size


def _seg_tiles(kind, name, n_tiles):
    return [(kind, COL[name] + n * TN) for n in range(n_tiles)]


MAIN_TILES = (_seg_tiles("aq", "a_q", 2) + _seg_tiles("cq", "c_q", 2) + _seg_tiles("dq", "d_q", 2)
              + _seg_tiles("plain", "a_k", 2) + _seg_tiles("plain", "a_v", 2)
              + _seg_tiles("dk", "d_k", 2) + _seg_tiles("plain", "d_v", 2) + _seg_tiles("ckv", "c_k", 1)
              + _seg_tiles("gate", "a_z", 2) + _seg_tiles("gate", "b_z", 2)
              + _seg_tiles("gate", "c_z", 2) + _seg_tiles("gate", "d_z", 2))


def _bq_kernel(*refs, lat, n_i):
    if lat:
        h_ref, w_ref, g_ref, wuq_ref, cosd, slod, shid, o_ref = refs[:8]
    else:
        h_ref, w_ref, g_ref, wuq_ref, o_ref = refs[:5]
    wb_ref, z0, z1, acc0, acc1 = refs[-5:]
    zs, accs = (z0, z1), (acc0, acc1)
    s = pl.program_id(0)

    def down(par):
        zs[par][...] = _dot_nt(h_ref[...], wb_ref[...])

    def up(par):
        cq = _rms(zs[par][...], g_ref[...]).astype(BF16)
        accs[par][...] = _dot(cq, wuq_ref[...])

    def epilogue(par):
        acc = accs[par]
        for hh in range(N_HEADS):
            c0 = hh * 2 * LANE
            o_ref[:, c0:c0 + LANE] = (acc[:, c0:c0 + LANE] * QS_B).astype(BF16)
            pe = acc[:, c0 + LANE:c0 + 2 * LANE]
            if lat:
                pe = _rope(pe, cosd[...], slod[...], shid[...], 16)
            o_ref[:, c0 + LANE:c0 + 2 * LANE] = (pe * QS_B).astype(BF16)

    @pl.when(s == 0)
    def _():
        wb_ref[...] = w_ref[...].astype(BF16)
        down(0)

    @pl.when(s == 1)
    def _():
        down(1)
        up(0)

    for par in range(2):
        @pl.when((s >= 2) & (s < n_i) & (s % 2 == par))
        def _(par=par):
            down(par)
            up(1 - par)
            epilogue(par)

    @pl.when(s == n_i)
    def _():
        up((n_i - 1) % 2)
        epilogue(n_i % 2)

    @pl.when(s == n_i + 1)
    def _():
        epilogue((n_i - 1) % 2)


def _bq_proj(h, w_t, layer, gain, w_uq, tables_d, lat):
    m = h.shape[0]
    tm = TM_B
    n_i = m // tm
    assert n_i >= 2
    done = lambda s: jnp.clip(s - 2, 0, n_i - 1)
    in_specs = [
        pl.BlockSpec((tm, D_MODEL), lambda s: (jnp.minimum(s, n_i - 1), 0)),
        _wt_spec(B_Q_LORA, lambda s: (layer, COL["b_cq"], 0)),
        pl.BlockSpec((1, B_Q_LORA), lambda s: (0, 0)),
        pl.BlockSpec((B_Q_LORA, N_HEADS * 2 * LANE), lambda s: (0, 0)),
    ]
    args = [h, w_t, gain, w_uq]
    if lat:
        t_blocks = tables_d[0].shape[0] // tm
        for t in tables_d:
            in_specs.append(pl.BlockSpec((tm, LANE), lambda s: (done(s) % t_blocks, 0)))
            args.append(t)
    return pl.pallas_call(
        functools.partial(_bq_kernel, lat=lat, n_i=n_i),
        out_shape=jax.ShapeDtypeStruct((m, N_HEADS * 2 * LANE), BF16),
        grid=(n_i + 2,),
        in_specs=in_specs,
        out_specs=pl.BlockSpec((tm, N_HEADS * 2 * LANE), lambda s: (done(s), 0)),
        scratch_shapes=[pltpu.VMEM((B_Q_LORA, D_MODEL), BF16),
                        pltpu.VMEM((tm, B_Q_LORA), F32), pltpu.VMEM((tm, B_Q_LORA), F32),
                        pltpu.VMEM((tm, N_HEADS * 2 * LANE), F32),
                        pltpu.VMEM((tm, N_HEADS * 2 * LANE), F32)],
        compiler_params=_cparams(1, 56),
        name="bq_proj_lat" if lat else "bq_proj_ctx",
    )(*args)


def _expand_kv(ckv, kpe, wukv_ref, kb_ref, vb_ref):
    kv = _dot(ckv.astype(BF16), wukv_ref[...])
    kpe_b = kpe.astype(BF16)
    for hh in range(N_HEADS):
        c0 = hh * 2 * LANE
        kb_ref[:, c0:c0 + LANE] = kv[:, c0:c0 + LANE].astype(BF16)
        kb_ref[:, c0 + LANE:c0 + 2 * LANE] = kpe_b
        vb_ref[:, hh * LANE:(hh + 1) * LANE] = kv[:, c0 + LANE:c0 + 2 * LANE].astype(BF16)


def _bkv_kernel(*refs, lat):
    if lat:
        h_ref, w_ref, wpe_ref, g_ref, wukv_ref, cosd, slod, shid, kb_ref, vb_ref, wb_ref, wpeb_ref = refs
    else:
        h_ref, w_ref, wpe_ref, g_ref, wukv_ref = refs[:5]
        kb_ref, vb_ref, ckv_ref, kpe_ref, wb_ref, wpeb_ref = refs[-6:]

    @pl.when(pl.program_id(0) == 0)
    def _():
        wb_ref[...] = w_ref[...].astype(BF16)
        wpeb_ref[...] = wpe_ref[...].astype(BF16)

    h = h_ref[...]
    ckv = _rms(_dot_nt(h, wb_ref[...]), g_ref[...])
    kpe = _dot_nt(h, wpeb_ref[...])
    lane = lax.broadcasted_iota(jnp.int32, (1, LANE), 1)
    kpe = jnp.where(lane < B_ROPE, kpe, 0.0)
    if lat:
        kpe = _rope(kpe, cosd[...], slod[...], shid[...], 16)
    else:
        ckv_ref[...] = ckv.reshape(ckv_ref.shape)
        kpe_ref[...] = kpe[:, :B_ROPE].reshape(kpe_ref.shape)
    _expand_kv(ckv, kpe, wukv_ref, kb_ref, vb_ref)


def _bkv_proj(h, w_t, layer, gain, w_ukv, tables_d, lat, seq=None, states=None):
    m = h.shape[0]
    tm = TM_B
    in_specs = [
        pl.BlockSpec((tm, D_MODEL), lambda i: (i, 0)),
        _wt_spec(B_KV_LORA, lambda i: (layer, COL["b_ckv"], 0)),
        _wt_spec(LANE, lambda i: (layer, COL["b_kpe"], 0)),
        pl.BlockSpec((1, B_KV_LORA), lambda i: (0, 0)),
        pl.BlockSpec((B_KV_LORA, N_HEADS * 2 * LANE), lambda i: (0, 0)),
    ]
    args = [h, w_t, w_t, gain, w_ukv]
    out_shape = [jax.ShapeDtypeStruct((m, N_HEADS * 2 * LANE), BF16),
                 jax.ShapeDtypeStruct((m, GROUP_W), BF16)]
    out_specs = [pl.BlockSpec((tm, N_HEADS * 2 * LANE), lambda i: (i, 0)),
                 pl.BlockSpec((tm, GROUP_W), lambda i: (i, 0))]
    if lat:
        t_blocks = tables_d[0].shape[0] // tm
        for t in tables_d:
            in_specs.append(pl.BlockSpec((tm, LANE), lambda i: (i % t_blocks, 0)))
            args.append(t)
    aliases = {}
    if not lat:
        out_shape += [jax.ShapeDtypeStruct((m // seq, DEPTH, seq, B_KV_LORA), F32),
                      jax.ShapeDtypeStruct((m // seq, DEPTH, seq, B_ROPE), F32)]
        out_specs += [pl.BlockSpec((tm // seq, None, seq, B_KV_LORA), lambda i: (i, layer, 0, 0)),
                      pl.BlockSpec((tm // seq, None, seq, B_ROPE), lambda i: (i, layer, 0, 0))]
        if states is not None:
            for n, st in enumerate(states):
                in_specs.append(pl.BlockSpec(memory_space=pl.ANY))
                aliases[len(args)] = 2 + n
                args.append(st)
    return pl.pallas_call(
        functools.partial(_bkv_kernel, lat=lat),
        out_shape=tuple(out_shape),
        input_output_aliases=aliases,
        grid=(m // tm,),
        in_specs=in_specs,
        out_specs=tuple(out_specs),
        scratch_shapes=[pltpu.VMEM((B_KV_LORA, D_MODEL), BF16), pltpu.VMEM((LANE, D_MODEL), BF16)],
        compiler_params=_cparams(1, 56),
        name="bkv_proj_lat" if lat else "bkv_proj_ctx",
    )(*args)


def _cache_kv_kernel(ckv_ref, kpe_ref, wukv_ref, kb_ref, vb_ref):
    _expand_kv(ckv_ref[...], kpe_ref[...], wukv_ref, kb_ref, vb_ref)


def _cache_kv(cache_ckv, cache_kpe128, w_ukv, layer):
    nb, _, s, _ = cache_ckv.shape
    return pl.pallas_call(
        _cache_kv_kernel,
        out_shape=(jax.ShapeDtypeStruct((nb * s, N_HEADS * 2 * LANE), BF16),
                   jax.ShapeDtypeStruct((nb * s, GROUP_W), BF16)),
        grid=(nb,),
        in_specs=[
            pl.BlockSpec((None, None, s, B_KV_LORA), lambda b: (b, layer, 0, 0)),
            pl.BlockSpec((None, None, s, LANE), lambda b: (b, layer, 0, 0)),
            pl.BlockSpec((B_KV_LORA, N_HEADS * 2 * LANE), lambda b: (0, 0)),
        ],
        out_specs=(pl.BlockSpec((s, N_HEADS * 2 * LANE), lambda b: (b, 0)),
                   pl.BlockSpec((s, GROUP_W), lambda b: (b, 0))),
        compiler_params=_cparams(1, 32),
        name="cache_kv",
    )(cache_ckv, cache_kpe128, w_ukv)


def _attn_kernel(*refs, n_src, nh, group, dk, n_chunks, diff, lam_init, k_staged):
    q_ref = refs[0]
    kv_refs = refs[1:1 + 2 * n_src]
    g_ref = refs[1 + 2 * n_src]
    pos = 2 + 2 * n_src
    if diff:
        lq1, lk1, lq2, lk2, subln = refs[pos:pos + 5]
        pos += 5
        lam = (jnp.exp(jnp.sum(lq1[...] * lk1[...], axis=-1, keepdims=True))
               - jnp.exp(jnp.sum(lq2[...] * lk2[...], axis=-1, keepdims=True)) + lam_init)
        lane = lax.broadcasted_iota(jnp.int32, (1, LANE), 1)
        m_lo = jnp.where(lane < D_QK, 1.0, 0.0).astype(BF16)
        m_hi = jnp.where(lane >= D_QK, 1.0, 0.0).astype(BF16)
    o_ref = refs[pos]
    scratch = list(refs[pos + 1:])
    n_kvh = nh // group

    k_refs, vt_refs = [], []
    for s in range(n_src):
        k_ref, v_ref = kv_refs[2 * s], kv_refs[2 * s + 1]
        if k_staged[s]:
            kb_ref = scratch.pop(0)
            kb_ref[...] = k_ref[...].astype(BF16)
            k_ref = kb_ref
        vt_ref = scratch.pop(0)
        for kh in range(n_kvh):
            v = v_ref[:, kh * LANE:(kh + 1) * LANE].astype(F32)
            vt_ref[kh * LANE:(kh + 1) * LANE, :] = v.T.astype(BF16)
        k_refs.append(k_ref)
        vt_refs.append(vt_ref)

    s_scr, p_scr, m_scr, l_scr = scratch[:4]
    n_w = 2 if diff else 1

    def rows_of(c):
        if isinstance(c, int):
            return slice(c * Q_CHUNK, (c + 1) * Q_CHUNK)
        return pl.ds(pl.multiple_of(c * Q_CHUNK, Q_CHUNK), Q_CHUNK)

    unroll = CHUNK_UNROLL if n_chunks % CHUNK_UNROLL == 0 else 1

    def split(c, j):
        sub, jj = divmod(j, nh * n_w)
        return (c * unroll + sub,) + divmod(jj, n_w)

    def qk(c, j):
        c, hh, w = split(c, j)
        kh = hh // group
        q = q_ref[rows_of(c), hh * dk:(hh + 1) * dk]
        if diff:
            q = q * (m_lo if w == 0 else m_hi)
        ks = [r[:, kh * dk:(kh + 1) * dk] for r in k_refs]
        _stage_qk(q, ks, [None] * n_src, s_scr.at[j % 2], m_scr.at[j % 2])

    def sm(c, j):
        _stage_softmax(s_scr.at[j % 2], m_scr.at[j % 2], p_scr.at[j % 2], l_scr.at[j % 2])

    def pv(c, j):
        c, hh, w = split(c, j)
        kh = hh // group
        vts = [r[kh * LANE:(kh + 1) * LANE, :] for r in vt_refs]
        o_t = _stage_pv(vts, p_scr.at[j % 2], l_scr.at[j % 2])
        if diff:
            o1_scr = scratch[4]
            if w == 0:
                o1_scr[...] = o_t
                return
            o = _rms((o1_scr[...] - lam * o_t).T, subln[...]) * (1.0 - lam_init)
        else:
            o = o_t.T
        rows = rows_of(c)
        osl = slice(hh * LANE, (hh + 1) * LANE)
        o_ref[rows, osl] = (o * g_ref[rows, osl]).astype(BF16)

    _pipeline(n_chunks // unroll, nh * n_w * unroll, qk, sm, pv)


def _attn(q, q_spec, srcs, g, g_spec, out_rows, out_spec, grid, *, nh, group, dk, t_q,
          diff_params=None, lam_init=0.0, name):
    in_specs = [q_spec]
    args = [q]
    scratch_shapes = []
    k_staged = []
    for ka, kspec, va, vspec in srcs:
        in_specs += [kspec, vspec]
        args += [ka, va]
        s_len, k_w = kspec.block_shape[-2:]
        v_w = vspec.block_shape[-1]
        k_staged.append(ka.dtype != BF16)
        if k_staged[-1]:
            scratch_shapes.append(pltpu.VMEM((s_len, k_w), BF16))
        scratch_shapes.append(pltpu.VMEM((v_w, s_len), BF16))
    in_specs.append(g_spec)
    args.append(g)
    s_total = sum(spec.block_shape[-2] for _, spec, _, _ in srcs)
    scratch_shapes += [pltpu.VMEM((2, s_total, Q_CHUNK), F32), pltpu.VMEM((2, s_total, Q_CHUNK), BF16),
                       pltpu.VMEM((2, 1, Q_CHUNK), F32), pltpu.VMEM((2, 1, Q_CHUNK), F32)]
    diff = diff_params is not None
    if diff:
        scratch_shapes.append(pltpu.VMEM((LANE, Q_CHUNK), F32))
        for p in diff_params:
            in_specs.append(pl.BlockSpec(p.shape, lambda *idx: (0, 0)))
            args.append(p)
    return pl.pallas_call(
        functools.partial(_attn_kernel, n_src=len(srcs), nh=nh, group=group, dk=dk,
                          n_chunks=t_q // Q_CHUNK, diff=diff, lam_init=lam_init,
                          k_staged=tuple(k_staged)),
        out_shape=jax.ShapeDtypeStruct((out_rows, GROUP_W), BF16),
        grid=grid,
        in_specs=in_specs,
        out_specs=out_spec,
        scratch_shapes=scratch_shapes,
        compiler_params=_cparams(len(grid), 56),
        name=name,
    )(*args)


NAT_ROWS_PER_CHUNK = Q_CHUNK // GRID_W
NAT_WIN_CHUNKS = 3
NAT_CHUNKS_PER_ITER = 4
NAT_N_DR = 2 * A_KH - 1
NAT_N_DC = 2 * A_KW - 1


def _natten_kernel(rpb_ref, q_ref, k_ref, v_ref, kc_ref, vc_ref, g_ref, o_ref,
                   tl_ref, tr_ref, vt_ref, kcb_ref, vct_ref, s_scr, p_scr, m_scr, l_scr, *, n_rows):
    h = pl.program_id(0)
    b = pl.program_id(1)
    n_chunks = q_ref.shape[0] // Q_CHUNK

    @pl.when(b == 0)
    def _build():
        ck = lax.broadcasted_iota(jnp.int32, (GRID_W, LANE), 0)
        lane = lax.broadcasted_iota(jnp.int32, (GRID_W, LANE), 1)
        left = lane < GRID_W
        cq = jnp.where(left, lane, lane - GRID_W)
        delta = jnp.clip(ck - cq, -(A_KW - 1), A_KW - 1) + (A_KW - 1)
        start_c = jnp.clip(cq - A_KW // 2, 0, GRID_W - A_KW)
        col_ok = (ck >= start_c) & (ck < start_c + A_KW)
        for d in range(NAT_N_DR):
            t = jnp.zeros((GRID_W, LANE), F32)
            for e in range(NAT_N_DC):
                t = jnp.where(delta == e, rpb_ref[h * (NAT_N_DR * NAT_N_DC) + d * NAT_N_DC + e], t)
            t = jnp.where(col_ok, t * LOG2E, MASK_NEG)
            tl_ref[d] = jnp.where(left, t, 0.0)
            tr_ref[d] = jnp.where(left, 0.0, t)
        tl_ref[NAT_N_DR] = jnp.where(left, MASK_NEG, 0.0)
        tr_ref[NAT_N_DR] = jnp.where(left, 0.0, MASK_NEG)

    for j in range(n_chunks):
        vt_ref[j] = v_ref[j * Q_CHUNK:(j + 1) * Q_CHUNK, :].astype(F32).T.astype(BF16)
    kcb_ref[...] = kc_ref[...].astype(BF16)
    vct_ref[...] = vc_ref[...].T.astype(BF16)
    kh = min(A_KH, n_rows)

    def chunk_of(ci, j):
        return NAT_CHUNKS_PER_ITER * ci + j

    def rows_of(c):
        if isinstance(c, int):
            return slice(c * Q_CHUNK, (c + 1) * Q_CHUNK)
        return pl.ds(pl.multiple_of(c * Q_CHUNK, Q_CHUNK), Q_CHUNK)

    def window(c):
        return jnp.clip(c - 1, 0, n_chunks - NAT_WIN_CHUNKS)

    def qk(ci, j):
        c = chunk_of(ci, j)
        r0 = c * NAT_ROWS_PER_CHUNK
        w0 = window(c)
        ws = w0 * NAT_ROWS_PER_CHUNK
        ks, biases = [], []
        for w in range(NAT_WIN_CHUNKS):
            ks.append(k_ref[rows_of(w0 + w), :])
            blk_rows = []
            for jk in range(NAT_ROWS_PER_CHUNK):
                rk = ws + w * NAT_ROWS_PER_CHUNK + jk
                blks = []
                for qp in range(NAT_ROWS_PER_CHUNK // 2):
                    idx = []
                    for half in range(2):
                        rq = r0 + 2 * qp + half
                        st = jnp.clip(rq - kh // 2, 0, n_rows - kh)
                        valid = (rk >= st) & (rk < st + kh)
                        idx.append(jnp.where(valid, rk - rq + (A_KH - 1), NAT_N_DR))
                    blks.append(tl_ref[idx[0]] + tr_ref[idx[1]])
                blk_rows.append(jnp.concatenate(blks, axis=1))
            biases.append(jnp.concatenate(blk_rows, axis=0))
        ks.append(kcb_ref[...])
        biases.append(None)
        _stage_qk(q_ref[rows_of(c), :], ks, biases, s_scr.at[j % 2], m_scr.at[j % 2])

    def sm(ci, j):
        _stage_softmax(s_scr.at[j % 2], m_scr.at[j % 2], p_scr.at[j % 2], l_scr.at[j % 2])

    def pv(ci, j):
        c = chunk_of(ci, j)
        w0 = window(c)
        vts = [vt_ref[w0 + w] for w in range(NAT_WIN_CHUNKS)] + [vct_ref[...]]
        o = _stage_pv(vts, p_scr.at[j % 2], l_scr.at[j % 2]).T
        rows = rows_of(c)
        o_ref[rows, :] = (o * g_ref[rows, :]).astype(BF16)

    _pipeline(n_chunks // NAT_CHUNKS_PER_ITER, NAT_CHUNKS_PER_ITER, qk, sm, pv)


def _natten(rpb_flat, zb, cache_k, cache_v, g, layer, nb, t):
    s_ctx = cache_k.shape[2]
    n_rows = t // GRID_W
    return pl.pallas_call(
        functools.partial(_natten_kernel, n_rows=n_rows),
        out_shape=jax.ShapeDtypeStruct((nb * t, GROUP_W), BF16),
        grid=(N_HEADS, nb),
        in_specs=[
            pl.BlockSpec(memory_space=pltpu.SMEM),
            pl.BlockSpec((t, LANE), lambda h, b: (b, h)),
            pl.BlockSpec((t, LANE), lambda h, b: (b, 24 + h)),
            pl.BlockSpec((t, LANE), lambda h, b: (b, 32 + h)),
            pl.BlockSpec((None, None, s_ctx, LANE), lambda h, b: (b, layer, 0, h)),
            pl.BlockSpec((None, None, s_ctx, LANE), lambda h, b: (b, layer, 0, h)),
            pl.BlockSpec((t, LANE), lambda h, b: (b, h)),
        ],
        out_specs=pl.BlockSpec((t, LANE), lambda h, b: (b, h)),
        scratch_shapes=[pltpu.VMEM((NAT_N_DR + 1, GRID_W, LANE), F32),
                        pltpu.VMEM((NAT_N_DR + 1, GRID_W, LANE), F32),
                        pltpu.VMEM((t // Q_CHUNK, LANE, Q_CHUNK), BF16),
                        pltpu.VMEM((s_ctx, LANE), BF16),
                        pltpu.VMEM((LANE, s_ctx), BF16),
                        pltpu.VMEM((2, NAT_WIN_CHUNKS * Q_CHUNK + s_ctx, Q_CHUNK), F32),
                        pltpu.VMEM((2, NAT_WIN_CHUNKS * Q_CHUNK + s_ctx, Q_CHUNK), BF16),
                        pltpu.VMEM((2, 1, Q_CHUNK), F32),
                        pltpu.VMEM((2, 1, Q_CHUNK), F32)],
        compiler_params=_cparams(2, 48),
        name="natten",
    )(rpb_flat, zb, zb, zb, cache_k, cache_v, g)


def _outproj_kernel(ya, yb, yc, yd, w_ref, x_ref, gate_ref, o_ref):
    acc = None
    for n, y in enumerate((ya, yb, yc, yd)):
        part = _dot(y[...], w_ref[n * GROUP_W:(n + 1) * GROUP_W, :].astype(BF16))
        acc = part if acc is None else acc + part
    o_ref[...] = x_ref[...] + gate_ref[...] * acc


def _outproj(ys, w_out, x, mods4, layer, row_fn):
    m = x.shape[0]
    gate_blk0 = 2 * D_MODEL // TN
    y_spec = pl.BlockSpec((TM, GROUP_W), lambda i, j: (i, 0))
    return pl.pallas_call(
        _outproj_kernel,
        out_shape=jax.ShapeDtypeStruct((m, D_MODEL), F32),
        grid=(m // TM, D_MODEL // TN),
        in_specs=[y_spec, y_spec, y_spec, y_spec,
                  pl.BlockSpec((None, D_MODEL, TN), lambda i, j: (layer, 0, j)),
                  pl.BlockSpec((TM, TN), lambda i, j: (i, j)),
                  pl.BlockSpec((None, None, 1, TN),
                               lambda i, j: (layer, row_fn(i * TM), 0, gate_blk0 + j))],
        out_specs=pl.BlockSpec((TM, TN), lambda i, j: (i, j)),
        compiler_params=_cparams(2, 56),
        name="outproj",
    )(*ys, w_out, x, mods4)


def _axial_tables(t, head_dim):
    pos = np.arange(t)
    part = head_dim // 2
    half = part // 2
    freqs = ROPE_THETA ** (-(np.arange(half, dtype=np.float64) * 2.0 / part))
    lane = np.arange(LANE) % head_dim
    p = np.where(lane < part, (pos // GRID_W)[:, None], (pos % GRID_W)[:, None])
    within = lane % part
    ang = p * freqs[within % half][None, :]
    first = (within < half)[None, :]
    cos = np.cos(ang)
    sin = np.sin(ang)
    return (jnp.asarray(cos, F32), jnp.asarray(np.where(first, -sin, 0.0), F32),
            jnp.asarray(np.where(first, 0.0, sin), F32))


def _ctx_layer(x, l, lam_init, w_t, w_out, w_uq, w_ukv, p, mods4, nb, t, prev_states):
    row_fn = lambda r: 0
    h = _norm(x, p["norm_g"], mods4, l, row_fn)
    zf_prev, b_prev = (None, None) if prev_states is None else (prev_states[0], prev_states[1:])
    zq, zf, g = _proj(h, w_t, l, MAIN_TILES, p["gains"], None, False, "main_proj_ctx",
                      seq=t, kv_state=zf_prev)
    qb = _bq_proj(h, w_t, l, p["b_q_norm"], w_uq, None, lat=False)
    kb, vb, ckv_state, kpe_state = _bkv_proj(h, w_t, l, p["b_kv_norm"], w_ukv, None, lat=False,
                                             seq=t, states=b_prev)

    grid = (nb,)
    wide = lambda blk: pl.BlockSpec((t, GROUP_W), lambda b: (b, blk))
    state = lambda w, blk: pl.BlockSpec((None, None, t, w), lambda b: (b, l, 0, blk))
    out_spec = pl.BlockSpec((t, GROUP_W), lambda b: (b, 0))
    m = nb * t
    ya = _attn(zq, wide(0), [(zf, state(GROUP_W, 0), zf, state(GROUP_W, 1))], g, wide(0), m, out_spec, grid,
               nh=N_HEADS, group=1, dk=LANE, t_q=t, name="attn_a_ctx")
    yb = _attn(qb, pl.BlockSpec((t, 2 * GROUP_W), lambda b: (b, 0)),
               [(kb, pl.BlockSpec((t, 2 * GROUP_W), lambda b: (b, 0)), vb, wide(0))],
               g, wide(1), m, out_spec, grid, nh=N_HEADS, group=1, dk=2 * LANE, t_q=t,
               name="attn_b_ctx")
    ckv_w = C_KV_HEADS * HEAD_DIM
    c_k_blk = 4 * GROUP_W // ckv_w
    yc = _attn(zq, wide(1), [(zf, state(ckv_w, c_k_blk), zf, state(ckv_w, c_k_blk + 1))],
               g, wide(2), m, out_spec, grid, nh=N_HEADS, group=N_HEADS // C_KV_HEADS,
               dk=LANE, t_q=t, name="attn_c_ctx")
    yd = _attn(zq, wide(2), [(zf, state(GROUP_W, 2), zf, state(GROUP_W, 3))], g, wide(3), m, out_spec, grid,
               nh=N_HEADS, group=1, dk=LANE, t_q=t,
               diff_params=p["diff"], lam_init=lam_init, name="attn_d_ctx")
    x_new = _outproj((ya, yb, yc, yd), w_out, x, mods4, l, row_fn)
    return x_new, (zf, ckv_state, kpe_state)


def _lat_layer(x, l, lam_init, w_t, w_out, w_uq, w_ukv, p, mods4, caches, rope_tab, tables_d, nb, t):
    row_fn = lambda r: 1 + r // t
    h = _norm(x, p["norm_g"], mods4, l, row_fn)
    zb, g = _proj(h, w_t, l, MAIN_TILES, p["gains"], rope_tab, True, "main_proj_lat")
    qb = _bq_proj(h, w_t, l, p["b_q_norm"], w_uq, tables_d, lat=True)
    kb, vb = _bkv_proj(h, w_t, l, p["b_kv_norm"], w_ukv, tables_d, lat=True)
    kbc, vbc = _cache_kv(caches["b_ckv"], caches["b_kpe"], w_ukv, l)
    s_ctx = caches["a_k"].shape[2]
    m = nb * t

    ya = _natten(p["rpb"], zb, caches["a_k"], caches["a_v"], g, l, nb, t)

    hps = LAT_HEADS_PER_STEP
    grid = (nb, N_HEADS // hps)
    col = lambda blk0, w=hps: pl.BlockSpec((t, w * LANE), lambda b, h: (b, blk0 // w + h))
    cache = lambda w=hps: pl.BlockSpec((None, None, s_ctx, w * LANE), lambda b, h: (b, l, 0, h))
    out_spec = col(0)
    hb = LAT_B_HEADS_PER_STEP
    yb = _attn(qb, col(0, 2 * hb),
               [(kb, col(0, 2 * hb), vb, col(0, hb)),
                (kbc, pl.BlockSpec((s_ctx, 2 * hb * LANE), lambda b, h: (b, h)),
                 vbc, pl.BlockSpec((s_ctx, hb * LANE), lambda b, h: (b, h)))],
               g, col(8, hb), m, col(0, hb), (nb, N_HEADS // hb), nh=hb, group=1, dk=2 * LANE, t_q=t,
               name="attn_b_lat")
    yd = _attn(zb, col(16),
               [(zb, col(40), zb, col(48)),
                (caches["d_k"], cache(), caches["d_v"], cache())],
               g, col(24), m, out_spec, grid, nh=hps, group=1, dk=LANE, t_q=t,
               diff_params=p["diff"], lam_init=lam_init, name="attn_d_lat")
    grp = N_HEADS // C_KV_HEADS
    yc = _attn(zb, col(8, grp),
               [(zb, col(56, 1), zb, col(58, 1)),
                (caches["c_k"], cache(1), caches["c_v"], cache(1))],
               g, col(16, grp), m, col(0, grp), (nb, C_KV_HEADS), nh=grp, group=grp, dk=LANE, t_q=t,
               name="attn_c_lat")
    return _outproj((ya, yb, yc, yd), w_out, x, mods4, l, row_fn)


def kernel(x_prompt, x_sample, cache_a_k, cache_a_v, cache_b_ckv, cache_b_kpe, cache_c_k, cache_c_v, cache_d_k, cache_d_v, c, c_ctx, norm_g, w_ada, b_ada, w_in, w_out, a_rpb, b_q_norm, b_w_uq, b_kv_norm, b_w_ukv, c_q_norm, c_k_norm, d_lq1, d_lk1, d_lq2, d_lk2, d_subln, final_norm_g):
    nb_c, t_c, _ = x_prompt.shape
    nb_l, t_l, _ = x_sample.shape
    s_ctx = cache_a_k.shape[2]

    c8 = jnp.concatenate([c_ctx[None, :], c, jnp.zeros((8 - 1 - nb_l, D_MODEL), F32)], axis=0)
    mods = _mods(c8, w_ada, b_ada)
    mods4 = mods[:, :1 + nb_l].reshape(DEPTH, 1 + nb_l, 1, 3 * D_MODEL)

    w_t = jnp.swapaxes(w_in, 1, 2)
    w_uq = jnp.pad(b_w_uq.reshape(DEPTH, B_Q_LORA, N_HEADS, B_NOPE + B_ROPE),
                   ((0, 0), (0, 0), (0, 0), (0, 2 * LANE - B_NOPE - B_ROPE)))
    w_uq = w_uq.reshape(DEPTH, B_Q_LORA, N_HEADS * 2 * LANE).astype(BF16)
    w_ukv = b_w_ukv.astype(BF16)

    caches = dict(
        a_k=cache_a_k.reshape(nb_l, DEPTH, s_ctx, GROUP_W),
        a_v=cache_a_v.reshape(nb_l, DEPTH, s_ctx, GROUP_W),
        b_ckv=cache_b_ckv,
        b_kpe=jnp.pad(cache_b_kpe, ((0, 0), (0, 0), (0, 0), (0, LANE - B_ROPE))),
        c_k=cache_c_k.reshape(nb_l, DEPTH, s_ctx, C_KV_HEADS * HEAD_DIM),
        c_v=cache_c_v.reshape(nb_l, DEPTH, s_ctx, C_KV_HEADS * HEAD_DIM),
        d_k=cache_d_k.reshape(nb_l, DEPTH, s_ctx, GROUP_W),
        d_v=cache_d_v.reshape(nb_l, DEPTH, s_ctx, GROUP_W),
    )
    tables_c = _axial_tables(t_l, HEAD_DIM)
    tables_d = _axial_tables(t_l, D_QK)
    no_rope = (jnp.ones((t_l, LANE), F32), jnp.zeros((t_l, LANE), F32), jnp.zeros((t_l, LANE), F32))
    rope_tab = jnp.stack(no_rope + tables_c + tables_d)

    xp = x_prompt.reshape(nb_c * t_c, D_MODEL)
    xs = x_sample.reshape(nb_l * t_l, D_MODEL)
    ctx_states = None
    for l in range(DEPTH):
        lam_init = 0.8 - 0.6 * math.exp(-0.3 * l)
        p = dict(
            norm_g=norm_g[l][None, :],
            gains=jnp.stack([jnp.ones((1, LANE), F32), c_q_norm[l][None, :], c_k_norm[l][None, :]]),
            b_q_norm=b_q_norm[l][None, :], b_kv_norm=b_kv_norm[l][None, :],
            rpb=a_rpb[l].reshape(-1),
            diff=(d_lq1[l][None, :], d_lk1[l][None, :], d_lq2[l][None, :], d_lk2[l][None, :],
                  d_subln[l][None, :]),
        )
        xp, ctx_states = _ctx_layer(xp, l, lam_init, w_t, w_out, w_uq[l], w_ukv[l], p, mods4, nb_c, t_c,
                                    ctx_states)
        xs = _lat_layer(xs, l, lam_init, w_t, w_out, w_uq[l], w_ukv[l], p, mods4, caches,
                        rope_tab, tables_d, nb_l, t_l)

    fg = final_norm_g[None, :]
    y_prompt = _norm(xp, fg, out_dtype=F32).reshape(nb_c, t_c, D_MODEL)
    y_sample = _norm(xs, fg, out_dtype=F32).reshape(nb_l, t_l, D_MODEL)

    zf, new_b_ckv, new_b_kpe = ctx_states
    ckv_w = C_KV_HEADS * HEAD_DIM

    def state(col0, width, tail):
        return zf[..., col0:col0 + width].reshape((nb_c, DEPTH, t_c) + tail)

    return (y_prompt, y_sample,
            state(0, GROUP_W, (N_HEADS, HEAD_DIM)), state(GROUP_W, GROUP_W, (N_HEADS, HEAD_DIM)),
            new_b_ckv, new_b_kpe,
            state(4 * GROUP_W, ckv_w, (C_KV_HEADS, HEAD_DIM)),
            state(4 * GROUP_W + ckv_w, ckv_w, (C_KV_HEADS, HEAD_DIM)),
            state(2 * GROUP_W, GROUP_W, (N_HEADS, 2, D_QK)), state(3 * GROUP_W, GROUP_W, (N_HEADS, HEAD_DIM)))
```
